```python
import math
import jax, jax.numpy as jnp
from jax import lax
import numpy as np

D_MODEL = 1024
BATCH = 4
SEQ = 8192
DEPTH = 2

MIX_WIDTH = D_MODEL
GDN_WIDTH = MIX_WIDTH // 2
GDN_HEAD_DIM = 128
GDN_HEADS = GDN_WIDTH // GDN_HEAD_DIM
CONV_K = 4
CHUNK = 64
S5_WIDTH = MIX_WIDTH - GDN_WIDTH
S5_GROUP = 16
S5_GROUPS = S5_WIDTH // S5_GROUP
S5_STATE = 64
D_FF = ((8 * D_MODEL // 3 + 127) // 128) * 128
N_MOD = 9
PROJ_WIDTH = 4 * GDN_WIDTH + 2 * GDN_HEADS + S5_WIDTH
EPS = 1e-6

kernel_name = "hybrid_gdn_s5_macaron_block"


def rms_norm(x, w):
    xf = x.astype(jnp.float32)
    y = xf * lax.rsqrt(jnp.mean(xf * xf, axis=-1, keepdims=True) + EPS)
    return (y * w.astype(jnp.float32)).astype(x.dtype)


def l2_normalize(t):
    return t * lax.rsqrt(jnp.sum(t * t, axis=-1, keepdims=True) + EPS)


def modulate(h, shift, scale):
    return h * (1.0 + scale) + shift


def swiglu(h, w_in, w_out):
    gate, up = jnp.split(h @ w_in, 2, axis=-1)
    return (jax.nn.silu(gate) * up) @ w_out


def causal_dwconv(x, w):
    K = w.shape[0]
    L = x.shape[1]
    xp = jnp.pad(x, ((0, 0), (K - 1, 0), (0, 0)))
    return sum(xp[:, j:j + L] * w[j] for j in range(K))


def chunked_gated_delta_rule(q, k, v, beta, g):
    Bsz, H, L, dk = q.shape
    dv = v.shape[-1]
    nc = L // CHUNK
    q = q * (dk ** -0.5)
    rs = lambda t: t.reshape(t.shape[:2] + (nc, CHUNK) + t.shape[3:])
    q, k, v, beta, g = rs(q), rs(k), rs(v), rs(beta), rs(g)
    g = jnp.cumsum(g, axis=-1)
    idx = jnp.arange(CHUNK)
    lower_incl = idx[:, None] >= idx[None, :]
    strict = idx[:, None] > idx[None, :]
    decay = jnp.exp(jnp.where(lower_incl, g[..., :, None] - g[..., None, :], -jnp.inf))
    kb = k * beta[..., None]
    vb = v * beta[..., None]
    lmat = jnp.einsum('bhnik,bhnjk->bhnij', kb, k) * decay * strict
    system = lmat + jnp.eye(CHUNK, dtype=lmat.dtype)
    rhs = jnp.concatenate([vb, kb * jnp.exp(g)[..., None]], axis=-1)
    sol = lax.linalg.triangular_solve(system, rhs, left_side=True, lower=True, unit_diagonal=True)
    w_val, k_cum = sol[..., :dv], sol[..., dv:]
    attn_intra = jnp.einsum('bhnik,bhnjk->bhnij', q, k) * decay
    g_last = g[..., -1]
    k_state = k * jnp.exp(g_last[..., None] - g)[..., None]
    q_state = q * jnp.exp(g)[..., None]

    def step(S, inp):
        qs, ks, kc, wv, attn, gl = inp
        v_new = wv - jnp.einsum('bhck,bhkv->bhcv', kc, S)
        o = jnp.einsum('bhck,bhkv->bhcv', qs, S) + jnp.einsum('bhij,bhjv->bhiv', attn, v_new)
        S = S * jnp.exp(gl)[..., None, None] + jnp.einsum('bhck,bhcv->bhkv', ks, v_new)
        return S, o

    xs = tuple(jnp.moveaxis(t, 2, 0) for t in (q_state, k_state, k_cum, w_val, attn_intra, g_last))
    S0 = jnp.zeros((Bsz, H, dk, dv), dtype=q.dtype)
    _, o = lax.scan(step, S0, xs)
    return jnp.moveaxis(o, 0, 2).reshape(Bsz, H, L, dv)


def _complex_linear_combine(earlier, later):
    ar1, ai1, br1, bi1 = earlier
    ar2, ai2, br2, bi2 = later
    ar = ar2 * ar1 - ai2 * ai1
    ai = ar2 * ai1 + ai2 * ar1
    br = ar2 * br1 - ai2 * bi1 + br2
    bi = ar2 * bi1 + ai2 * br1 + bi2
    return (ar, ai, br, bi)


def s5_mixer(u, a_re, a_im, log_dt, b_re, b_im, c_re, c_im, d, w_glu):
    f32 = jnp.float32
    Bsz, L, _ = u.shape
    u = u.astype(f32).reshape(Bsz, L, S5_GROUPS, S5_GROUP)
    ar = jnp.minimum(a_re.astype(f32), -1e-4)
    ai = a_im.astype(f32)
    dt = jnp.exp(log_dt.astype(f32))[:, None]
    mag = jnp.exp(dt * ar)
    abar_re = mag * jnp.cos(dt * ai)
    abar_im = mag * jnp.sin(dt * ai)
    denom = ar * ar + ai * ai
    zr = abar_re - 1.0
    zi = abar_im
    fr = (zr * ar + zi * ai) / denom
    fi = (zi * ar - zr * ai) / denom
    br = b_re.astype(f32)
    bi = b_im.astype(f32)
    bbar_re = fr[..., None] * br - fi[..., None] * bi
    bbar_im = fr[..., None] * bi + fi[..., None] * br
    bu_re = jnp.einsum('gph,blgh->blgp', bbar_re, u)
    bu_im = jnp.einsum('gph,blgh->blgp', bbar_im, u)
    seq_shape = (1, L, S5_GROUPS, S5_STATE)
    a_seq_re = jnp.broadcast_to(abar_re, seq_shape)
    a_seq_im = jnp.broadcast_to(abar_im, seq_shape)
    _, _, xr, xi = lax.associative_scan(_complex_linear_combine, (a_seq_re, a_seq_im, bu_re, bu_im), axis=1)
    y = (jnp.einsum('ghp,blgp->blgh', c_re.astype(f32), xr)
         - jnp.einsum('ghp,blgp->blgh', c_im.astype(f32), xi)
         + d.astype(f32) * u)
    y = jax.nn.gelu(y.reshape(Bsz, L, S5_WIDTH))
    return y * jax.nn.sigmoid(y @ w_glu.astype(f32))


def hybrid_mixer(h, w_in, conv_w, a_log, dt_bias, gdn_norm_w,
                 s5_a_re, s5_a_im, s5_log_dt, s5_b_re, s5_b_im, s5_c_re, s5_c_im, s5_d, s5_w_glu, w_out):
    f32 = jnp.float32
    Bsz, L, _ = h.shape
    proj = h @ w_in
    qkv, z, beta_in, a_in, u = jnp.split(
        proj, [3 * GDN_WIDTH, 4 * GDN_WIDTH, 4 * GDN_WIDTH + GDN_HEADS, 4 * GDN_WIDTH + 2 * GDN_HEADS], axis=-1)
    qkv = jax.nn.silu(causal_dwconv(qkv, conv_w))
    q, k, v = jnp.split(qkv, 3, axis=-1)
    heads = lambda t: t.reshape(Bsz, L, GDN_HEADS, GDN_HEAD_DIM).transpose(0, 2, 1, 3).astype(f32)
    q = l2_normalize(heads(q))
    k = l2_normalize(heads(k))
    v = heads(v)
    beta = jax.nn.sigmoid(beta_in.astype(f32)).transpose(0, 2, 1)
    g = (-jnp.exp(a_log.astype(f32)) * jax.nn.softplus(a_in.astype(f32) + dt_bias.astype(f32))).transpose(0, 2, 1)
    o = chunked_gated_delta_rule(q, k, v, beta, g).transpose(0, 2, 1, 3)
    o = rms_norm(o, gdn_norm_w) * jax.nn.silu(z.astype(f32).reshape(Bsz, L, GDN_HEADS, GDN_HEAD_DIM))
    y_gdn = o.reshape(Bsz, L, GDN_WIDTH)
    y_s5 = s5_mixer(u, s5_a_re, s5_a_im, s5_log_dt, s5_b_re, s5_b_im, s5_c_re, s5_c_im, s5_d, s5_w_glu)
    y = jnp.concatenate([y_gdn.astype(h.dtype), y_s5.astype(h.dtype)], axis=-1)
    return y @ w_out


def setup_inputs(seed: int = 0) -> dict:
    key = jax.random.key(seed)
    ks = jax.random.split(key, 32)
    nrm = lambda k, shape, s: jax.random.normal(k, shape, jnp.float32) * s
    gain = lambda k, shape: 1.0 + 0.05 * jax.random.normal(k, shape, jnp.float32)
    L_ = DEPTH
    dt_gdn = jnp.exp(jax.random.uniform(ks[14], (L_, GDN_HEADS), jnp.float32, math.log(1e-3), math.log(1e-1)))
    return {
        "x": nrm(ks[0], (BATCH, SEQ, D_MODEL), 1.0),
        "c": nrm(ks[1], (BATCH, D_MODEL), 1.0),
        "w_mod": nrm(ks[2], (L_, D_MODEL, N_MOD * D_MODEL), 0.5 * D_MODEL ** -0.5),
        "b_mod": nrm(ks[3], (L_, N_MOD * D_MODEL), 0.01),
        "ff1_norm_pre": gain(ks[4], (L_, D_MODEL)),
        "ff1_norm_post": gain(ks[5], (L_, D_MODEL)),
        "ff1_w_in": nrm(ks[6], (L_, D_MODEL, 2 * D_FF), D_MODEL ** -0.5),
        "ff1_w_out": nrm(ks[7], (L_, D_FF, D_MODEL), D_FF ** -0.5),
        "mix_norm_pre": gain(ks[8], (L_, D_MODEL)),
        "mix_norm_post": gain(ks[9], (L_, D_MODEL)),
        "mix_w_in": nrm(ks[10], (L_, D_MODEL, PROJ_WIDTH), D_MODEL ** -0.5),
        "conv_w": nrm(ks[11], (L_, CONV_K, 3 * GDN_WIDTH), CONV_K ** -0.5),
        "a_log": jnp.log(jax.random.uniform(ks[12], (L_, GDN_HEADS), jnp.float32, 1.0, 16.0)),
        "dt_bias": dt_gdn + jnp.log(-jnp.expm1(-dt_gdn)),
        "gdn_norm_w": gain(ks[13], (L_, GDN_HEAD_DIM)),
        "s5_a_re": -0.5 + 0.01 * jax.random.normal(ks[15], (L_, S5_GROUPS, S5_STATE), jnp.float32),
        "s5_a_im": jnp.broadcast_to(math.pi * jnp.arange(S5_STATE, dtype=jnp.float32), (L_, S5_GROUPS, S5_STATE)),
        "s5_log_dt": jax.random.uniform(ks[16], (L_, S5_GROUPS), jnp.float32, math.log(1e-3), math.log(1e-1)),
        "s5_b_re": nrm(ks[17], (L_, S5_GROUPS, S5_STATE, S5_GROUP), (2 * S5_GROUP) ** -0.5),
        "s5_b_im": nrm(ks[18], (L_, S5_GROUPS, S5_STATE, S5_GROUP), (2 * S5_GROUP) ** -0.5),
        "s5_c_re": nrm(ks[19], (L_, S5_GROUPS, S5_GROUP, S5_STATE), (2 * S5_STATE) ** -0.5),
        "s5_c_im": nrm(ks[20], (L_, S5_GROUPS, S5_GROUP, S5_STATE), (2 * S5_STATE) ** -0.5),
        "s5_d": nrm(ks[21], (L_, S5_GROUPS, S5_GROUP), 1.0),
        "s5_w_glu": nrm(ks[22], (L_, S5_WIDTH, S5_WIDTH), S5_WIDTH ** -0.5),
        "mix_w_out": nrm(ks[23], (L_, MIX_WIDTH, D_MODEL), MIX_WIDTH ** -0.5),
        "ff2_norm_pre": gain(ks[24], (L_, D_MODEL)),
        "ff2_norm_post": gain(ks[25], (L_, D_MODEL)),
        "ff2_w_in": nrm(ks[26], (L_, D_MODEL, 2 * D_FF), D_MODEL ** -0.5),
        "ff2_w_out": nrm(ks[27], (L_, D_FF, D_MODEL), D_FF ** -0.5),
    }


def reference(x, c, w_mod, b_mod, ff1_norm_pre, ff1_norm_post, ff1_w_in, ff1_w_out,
              mix_norm_pre, mix_norm_post, mix_w_in, conv_w, a_log, dt_bias, gdn_norm_w,
              s5_a_re, s5_a_im, s5_log_dt, s5_b_re, s5_b_im, s5_c_re, s5_c_im, s5_d, s5_w_glu, mix_w_out,
              ff2_norm_pre, ff2_norm_post, ff2_w_in, ff2_w_out):
    c_act = jax.nn.silu(c)
    for l in range(DEPTH):
        mod = (c_act @ w_mod[l] + b_mod[l])[:, None, :]
        sh1, sc1, gt1, sh2, sc2, gt2, sh3, sc3, gt3 = jnp.split(mod, N_MOD, axis=-1)
        h = modulate(rms_norm(x, ff1_norm_pre[l]), sh1, sc1)
        x = x + 0.5 * gt1 * rms_norm(swiglu(h, ff1_w_in[l], ff1_w_out[l]), ff1_norm_post[l])
        h = modulate(rms_norm(x, mix_norm_pre[l]), sh2, sc2)
        y = hybrid_mixer(h, mix_w_in[l], conv_w[l], a_log[l], dt_bias[l], gdn_norm_w[l],
                         s5_a_re[l], s5_a_im[l], s5_log_dt[l], s5_b_re[l], s5_b_im[l],
                         s5_c_re[l], s5_c_im[l], s5_d[l], s5_w_glu[l], mix_w_out[l])
        x = x + gt2 * rms_norm(y, mix_norm_post[l])
        h = modulate(rms_norm(x, ff2_norm_pre[l]), sh3, sc3)
        x = x + 0.5 * gt3 * rms_norm(swiglu(h, ff2_w_in[l], ff2_w_out[l]), ff2_norm_post[l])
    return x
```

```python
import functools
import math

import jax
import jax.numpy as jnp
from jax import lax
from jax.experimental import pallas as pl
from jax.experimental.pallas import tpu as pltpu

F32 = jnp.float32
BF16 = jnp.bfloat16
EPS = 1e-6

HEAD_DIM = 128
GDN_CHUNK = 64
CONV_K = 4
S5_GROUP = 16
S5_STATE = 64
N_MOD = 9
S5_T = 32
BG_LANES = 128

V7X_VMEM_LIMIT_BYTES = 56 * 1024 * 1024
HIGHEST = lax.Precision.HIGHEST


def _params(semantics):
    return pltpu.CompilerParams(dimension_semantics=semantics, vmem_limit_bytes=V7X_VMEM_LIMIT_BYTES)


def _resident(block_shape, index_map):
    return pl.BlockSpec(block_shape, index_map, pipeline_mode=pl.Buffered(1))


def _dot(a, b):
    return jnp.dot(a.astype(BF16), b.astype(BF16), preferred_element_type=F32)


def _dot_nt(a, b):
    return lax.dot_general(a.astype(BF16), b.astype(BF16), (((1,), (1,)), ((), ())),
                           preferred_element_type=F32)


def _dot_tn(a, b):
    return lax.dot_general(a.astype(BF16), b.astype(BF16), (((0,), (0,)), ((), ())),
                           preferred_element_type=F32)


def _dot_split(a, b):
    a_hi = a.astype(BF16)
    a_lo = (a - a_hi.astype(F32)).astype(BF16)
    b_hi = b.astype(BF16)
    b_lo = (b - b_hi.astype(F32)).astype(BF16)
    mm = functools.partial(jnp.dot, preferred_element_type=F32)
    return mm(a_hi, b_hi) + mm(a_hi, b_lo) + mm(a_lo, b_hi)


def _dot_f32(a, b):
    return jnp.dot(a, b, precision=HIGHEST, preferred_element_type=F32)


def _rms(x, w):
    return x * lax.rsqrt(jnp.mean(x * x, axis=-1, keepdims=True) + EPS) * w


def _silu(x):
    return x * jax.nn.sigmoid(x)


def _div_pow2(x, n):
    assert n & (n - 1) == 0
    return jnp.right_shift(x, n.bit_length() - 1)


def _mod_pow2(x, n):
    assert n & (n - 1) == 0
    return jnp.bitwise_and(x, n - 1)


def _mod_kernel(c_ref, w_ref, b_ref, o_ref):
    c = c_ref[...]
    o_ref[0] = _dot(_silu(c), w_ref[0]) + b_ref[0]


def _modulation(c, w_mod, b_mod):
    depth, d, n = w_mod.shape
    bsz = c.shape[0]
    rows = 8
    assert bsz <= rows
    tn = n // 8
    c_pad = jnp.pad(c, ((0, rows - bsz), (0, 0)))
    return pl.pallas_call(
        _mod_kernel,
        grid=(depth, n // tn),
        in_specs=[
            pl.BlockSpec((rows, d), lambda l, j: (0, 0)),
            pl.BlockSpec((1, d, tn), lambda l, j: (l, 0, j)),
            pl.BlockSpec((1, 1, tn), lambda l, j: (l, 0, j)),
        ],
        out_specs=pl.BlockSpec((1, rows, tn), lambda l, j: (l, 0, j)),
        out_shape=jax.ShapeDtypeStruct((depth, rows, n), F32),
        compiler_params=_params(("arbitrary", "arbitrary")),
        name="adaln_mod",
    )(c_pad, w_mod, b_mod.reshape(depth, 1, n))


def _mod_specs(sub, d):
    return [pl.BlockSpec((8, d), functools.partial(lambda b, i, j: (0, j), j=3 * sub + k)) for k in range(3)]


def _ffn_kernel(x_ref, sh_ref, sc_ref, gt_ref, npre_ref, npost_ref, win_ref, wout_ref, o_ref, acc_ref,
                *, n_chunks, tf):
    b = pl.program_id(0)
    x = x_ref[0]
    sh = sh_ref[pl.ds(b, 1), :]
    sc = sc_ref[pl.ds(b, 1), :]
    gt = gt_ref[pl.ds(b, 1), :]
    h = (_rms(x, npre_ref[...]) * (1.0 + sc) + sh).astype(BF16)
    acc_ref[...] = jnp.zeros_like(acc_ref)

    def chunk(ci, carry):
        gu = jnp.dot(h, win_ref[ci], preferred_element_type=F32)
        act = _silu(gu[:, :tf]) * gu[:, tf:]
        acc_ref[...] += _dot(act, wout_ref[ci])
        return carry

    lax.fori_loop(0, n_chunks, chunk, 0)
    o_ref[0] = x + (0.5 * gt) * _rms(acc_ref[...], npost_ref[...])


def _ffn_weights(w_in, w_out, tf):
    d, two_f = w_in.shape
    f = two_f // 2
    n = f // tf
    gate = w_in[:, :f].reshape(d, n, tf)
    up = w_in[:, f:].reshape(d, n, tf)
    win = jnp.concatenate([gate, up], axis=2).transpose(1, 0, 2).astype(BF16)
    wout = w_out.reshape(n, tf, d).astype(BF16)
    return win, wout


def _ffn(x, mod, sub, npre, npost, w_in, w_out, tm):
    bsz, seq, d = x.shape
    f = w_out.shape[0]
    tf = 256
    assert f % tf == 0 and seq % tm == 0
    n = f // tf
    win, wout = _ffn_weights(w_in, w_out, tf)
    return pl.pallas_call(
        functools.partial(_ffn_kernel, n_chunks=n, tf=tf),
        grid=(bsz, seq // tm),
        in_specs=[pl.BlockSpec((1, tm, d), lambda b, i: (b, i, 0))] + _mod_specs(sub, d) + [
            pl.BlockSpec((1, d), lambda b, i: (0, 0)),
            pl.BlockSpec((1, d), lambda b, i: (0, 0)),
            _resident((n, d, 2 * tf), lambda b, i: (0, 0, 0)),
            _resident((n, tf, d), lambda b, i: (0, 0, 0)),
        ],
        out_specs=pl.BlockSpec((1, tm, d), lambda b, i: (b, i, 0)),
        out_shape=jax.ShapeDtypeStruct(x.shape, F32),
        scratch_shapes=[pltpu.VMEM((tm, d), F32)],
        compiler_params=_params(("arbitrary", "arbitrary")),
        name="ffn",
    )(x, mod, mod, mod, npre.reshape(1, d), npost.reshape(1, d), win, wout)


def _mix_in_kernel(x_ref, sh_ref, sc_ref, npre_ref, w_ref, conv_ref, alog_ref, dtb_ref,
                   q_ref, k_ref, v_ref, z_ref, u_ref, bg_ref, ext_ref, *, heads, gw):
    b = pl.program_id(0)
    i = pl.program_id(1)
    tm = x_ref.shape[1]
    x = x_ref[0]
    sh = sh_ref[pl.ds(b, 1), :]
    sc = sc_ref[pl.ds(b, 1), :]
    h = _rms(x, npre_ref[...]) * (1.0 + sc) + sh
    proj = _dot(h, w_ref[...])

    @pl.when(i == 0)
    def _():
        ext_ref[0:8, :] = jnp.zeros((8, 3 * gw), F32)

    ext_ref[8:, :] = proj[:, :3 * gw]
    conv = conv_ref[0:1, :] * ext_ref[pl.ds(8 - (CONV_K - 1), tm), :]
    for j in range(1, CONV_K):
        conv = conv + conv_ref[j:j + 1, :] * ext_ref[pl.ds(8 - (CONV_K - 1) + j, tm), :]
    ext_ref[0:8, :] = ext_ref[tm:tm + 8, :]
    qkv = _silu(conv)

    for hd in range(heads):
        lo = hd * HEAD_DIM
        for base, ref in ((0, q_ref), (gw, k_ref)):
            t = qkv[:, base + lo:base + lo + HEAD_DIM]
            ref[0, :, lo:lo + HEAD_DIM] = t * lax.rsqrt(jnp.sum(t * t, axis=-1, keepdims=True) + EPS)
    v_ref[0] = qkv[:, 2 * gw:3 * gw]
    z_ref[0] = proj[:, 3 * gw:4 * gw]
    sw = u_ref.shape[2]
    u_ref[0] = proj[:, 4 * gw:4 * gw + sw]

    ba = proj[:, 4 * gw + sw:]
    beta = jax.nn.sigmoid(ba)
    t = ba + dtb_ref[...]
    softplus = jnp.maximum(t, 0.0) + jnp.log1p(jnp.exp(-jnp.abs(t)))
    g = -jnp.exp(alog_ref[...]) * softplus
    lane = lax.broadcasted_iota(jnp.int32, ba.shape, 1)
    bg_ref[0] = jnp.where(lane < heads, beta, g)


def _mix_in(x, mod, npre, w_in, conv_w, a_log, dt_bias, tm):
    bsz, seq, d = x.shape
    gw = d // 2
    sw = d - gw
    heads = gw // HEAD_DIM
    assert 2 * heads <= BG_LANES and seq % tm == 0
    qkv_w, z_w, beta_w, a_w, u_w = jnp.split(
        w_in, [3 * gw, 4 * gw, 4 * gw + heads, 4 * gw + 2 * heads], axis=1)
    pad = jnp.zeros((d, BG_LANES - 2 * heads), w_in.dtype)
    w = jnp.concatenate([qkv_w, z_w, u_w, beta_w, a_w, pad], axis=1).astype(BF16)
    n = w.shape[1]
    lane_pad = (0, BG_LANES - 2 * heads)
    alog = jnp.pad(jnp.concatenate([jnp.zeros_like(a_log), a_log]), lane_pad).reshape(1, BG_LANES)
    dtb = jnp.pad(jnp.concatenate([jnp.zeros_like(dt_bias), dt_bias]), lane_pad).reshape(1, BG_LANES)
    tok = lambda width: pl.BlockSpec((1, tm, width), lambda b, i: (b, i, 0))
    shp = lambda width: jax.ShapeDtypeStruct((bsz, seq, width), F32)
    return pl.pallas_call(
        functools.partial(_mix_in_kernel, heads=heads, gw=gw),
        grid=(bsz, seq // tm),
        in_specs=[tok(d)] + _mod_specs(1, d)[:2] + [
            pl.BlockSpec((1, d), lambda b, i: (0, 0)),
            _resident((d, n), lambda b, i: (0, 0)),
            pl.BlockSpec((CONV_K, 3 * gw), lambda b, i: (0, 0)),
            pl.BlockSpec((1, BG_LANES), lambda b, i: (0, 0)),
            pl.BlockSpec((1, BG_LANES), lambda b, i: (0, 0)),
        ],
        out_specs=[tok(gw), tok(gw), tok(gw), tok(gw), tok(sw), tok(BG_LANES)],
        out_shape=[shp(gw), shp(gw), shp(gw), shp(gw), shp(sw), shp(BG_LANES)],
        scratch_shapes=[pltpu.VMEM((tm + 8, 3 * gw), F32)],
        compiler_params=_params(("arbitrary", "arbitrary")),
        name="mix_in",
    )(x, mod, mod, npre.reshape(1, d), w, conv_w, alog, dtb)


def _gdn_kernel(q_ref, k_ref, v_ref, z_ref, bg_ref, nw_ref, y_ref, s_ref, *, heads, n_chunks):
    c = GDN_CHUNK

    @pl.when(pl.program_id(1) == 0)
    def _():
        s_ref[...] = jnp.zeros_like(s_ref)

    row = lax.broadcasted_iota(jnp.int32, (c, c), 0)
    col = lax.broadcasted_iota(jnp.int32, (c, c), 1)
    lower_incl = row >= col
    strict = row > col
    ltri = lower_incl.astype(F32)
    utri = (row <= col).astype(F32)
    eye = (row == col).astype(F32)
    ones_c = jnp.ones((c, c), F32)
    ones_hd = jnp.ones((HEAD_DIM, c), F32)
    scale = HEAD_DIM ** -0.5
    nw = nw_ref[...]

    for ci in range(n_chunks):
        rows = slice(ci * c, (ci + 1) * c)
        bg = bg_ref[0, rows, :]
        gc_all = _dot_f32(ltri, bg)
        gl_all = _dot_f32(ones_hd, bg)
        for hd in range(heads):
            hs = slice(hd * HEAD_DIM, (hd + 1) * HEAD_DIM)
            beta = bg[:, hd:hd + 1]
            g = bg[:, heads + hd:heads + hd + 1]
            gc = gc_all[:, heads + hd:heads + hd + 1]
            gl = gl_all[:, heads + hd:heads + hd + 1]
            gc_row = _dot_f32(ones_c, g * utri)
            decay = jnp.exp(jnp.where(lower_incl, gc - gc_row, -jnp.inf))
            q = q_ref[0, rows, hs] * scale
            k = k_ref[0, rows, hs]
            v = v_ref[0, rows, hs]
            kb = k * beta
            vb = v * beta
            a_mat = jnp.where(strict, -(_dot_nt(kb, k) * decay), 0.0)
            a_pow = a_mat
            inv = eye + a_pow
            for _ in range(int(math.log2(c)) - 1):
                a_pow = _dot(a_pow, a_pow)
                inv = inv + _dot(inv, a_pow)
            inv = inv + _dot(inv, (eye - inv) + _dot_split(a_mat, inv))
            eg = jnp.exp(gc)
            sol = _dot(inv, jnp.concatenate([vb, kb * eg], axis=-1))
            w_val = sol[:, :HEAD_DIM]
            k_cum = sol[:, HEAD_DIM:]
            attn = _dot_nt(q, k) * decay
            k_state = k * jnp.exp(gl[:c] - gc)
            q_state = q * eg

            state = s_ref[hd]
            v_new = w_val - _dot(k_cum, state)
            o = _dot(q_state, state) + _dot(attn, v_new)
            s_ref[hd] = state * jnp.exp(gl) + _dot_tn(k_state, v_new)

            z = z_ref[0, rows, hs]
            y_ref[0, rows, hs] = _rms(o, nw) * _silu(z)


def _gdn(q, k, v, z, bg, norm_w, tl):
    bsz, seq, gw = q.shape
    heads = gw // HEAD_DIM
    assert seq % tl == 0 and tl % GDN_CHUNK == 0
    tok = lambda width: pl.BlockSpec((1, tl, width), lambda b, i: (b, i, 0))
    return pl.pallas_call(
        functools.partial(_gdn_kernel, heads=heads, n_chunks=tl // GDN_CHUNK),
        grid=(bsz, seq // tl),
        in_specs=[tok(gw), tok(gw), tok(gw), tok(gw), tok(BG_LANES),
                  pl.BlockSpec((1, HEAD_DIM), lambda b, i: (0, 0))],
        out_specs=tok(gw),
        out_shape=jax.ShapeDtypeStruct((bsz, seq, gw), F32),
        scratch_shapes=[pltpu.VMEM((heads, HEAD_DIM, HEAD_DIM), F32)],
        compiler_params=_params(("arbitrary", "arbitrary")),
        name="gdn",
    )(q, k, v, z, bg, norm_w.reshape(1, HEAD_DIM))


def _s5_param_kernel(are_ref, aim_ref, ldt_ref, brt_ref, bit_ref, cr_ref, ci_ref,
                     kf_ref, p_ref, g1_ref, mul_ref, *, n_levels):
    t_len = S5_T
    ns = S5_STATE
    hw = S5_GROUP
    ar = jnp.minimum(are_ref[0], -1e-4)
    ai = aim_ref[0]
    dt = jnp.exp(ldt_ref[0])
    mag = jnp.exp(dt * ar)
    abar_re = mag * jnp.cos(dt * ai)
    abar_im = mag * jnp.sin(dt * ai)
    denom = ar * ar + ai * ai
    zr = abar_re - 1.0
    zi = abar_im
    fr = (zr * ar + zi * ai) / denom
    fi = (zi * ar - zr * ai) / denom
    brt = brt_ref[0]
    bit = bit_ref[0]
    bbar_re_t = fr * brt - fi * bit
    bbar_im_t = fr * bit + fi * brt

    n_tab = t_len + 8
    tau = lax.broadcasted_iota(jnp.int32, (n_tab, ns), 0).astype(F32)
    tab_mag = jnp.exp(tau * (dt * ar))
    lam_re = tab_mag * jnp.cos(tau * (dt * ai))
    lam_im = tab_mag * jnp.sin(tau * (dt * ai))

    rows = t_len * hw
    r_tau = _div_pow2(lax.broadcasted_iota(jnp.int32, (rows, n_tab), 0), hw)
    c_tau = lax.broadcasted_iota(jnp.int32, (rows, n_tab), 1)
    rep0 = (c_tau == r_tau).astype(F32)
    rep1 = (c_tau == r_tau + 1).astype(F32)
    rep_rev = (c_tau == t_len - 1 - r_tau).astype(F32)
    r_h = _mod_pow2(lax.broadcasted_iota(jnp.int32, (rows, hw), 0), hw)
    c_h = lax.broadcasted_iota(jnp.int32, (rows, hw), 1)
    tile = (r_h == c_h).astype(F32)

    c_re_t = _dot_f32(tile, cr_ref[0])
    c_im_t = _dot_f32(tile, ci_ref[0])
    b_re_t = _dot_f32(tile, bbar_re_t)
    b_im_t = _dot_f32(tile, bbar_im_t)

    def c_lam(rep):
        lr = _dot_f32(rep, lam_re)
        li = _dot_f32(rep, lam_im)
        return jnp.concatenate([c_re_t * lr - c_im_t * li, -(c_re_t * li + c_im_t * lr)], axis=-1)

    bmat_t = jnp.concatenate([bbar_re_t, bbar_im_t], axis=-1)
    kf_ref[0] = lax.dot_general(c_lam(rep0), bmat_t, (((1,), (1,)), ((), ())),
                                precision=HIGHEST, preferred_element_type=F32)
    g1_ref[0] = c_lam(rep1)
    lr = _dot_f32(rep_rev, lam_re)
    li = _dot_f32(rep_rev, lam_im)
    p_ref[0] = jnp.concatenate([lr * b_re_t - li * b_im_t, lr * b_im_t + li * b_re_t], axis=-1)

    step_re = lam_re[t_len:t_len + 1, :]
    step_im = lam_im[t_len:t_len + 1, :]
    for lvl in range(n_levels):
        mul_ref[0, 2 * lvl:2 * lvl + 1, :] = jnp.concatenate([step_re, step_re], axis=-1)
        mul_ref[0, 2 * lvl + 1:2 * lvl + 2, :] = jnp.concatenate([-step_im, step_im], axis=-1)
        step_re, step_im = step_re * step_re - step_im * step_im, 2.0 * step_re * step_im


def _s5_main_kernel(u_ref, w_ref, g1_ref, mul_ref, d_ref, y_ref, *, n_chunks, n_levels):
    tw = u_ref.shape[2]
    u = u_ref[0]
    yz = _dot(u, w_ref[0])
    state = yz[:, tw:]
    chunk = _mod_pow2(lax.broadcasted_iota(jnp.int32, state.shape, 0), n_chunks)
    for lvl in range(n_levels):
        dist = 2 ** lvl
        prev = jnp.where(chunk >= dist, pltpu.roll(state, dist, axis=0), 0.0)
        state = (state + prev * mul_ref[0, 2 * lvl:2 * lvl + 1, :]
                 + pltpu.roll(prev, S5_STATE, axis=1) * mul_ref[0, 2 * lvl + 1:2 * lvl + 2, :])
    incoming = jnp.where(chunk >= 1, pltpu.roll(state, 1, axis=0), 0.0)
    y = yz[:, :tw] + _dot_nt(incoming, g1_ref[0]) + d_ref[0] * u
    y_ref[0] = jax.nn.gelu(y)


def _s5(u, a_re, a_im, log_dt, b_re, b_im, c_re, c_im, d_skip):
    bsz, seq, sw = u.shape
    groups = sw // S5_GROUP
    t_len = S5_T
    assert seq % t_len == 0
    nc = seq // t_len
    n_levels = max(1, int(math.ceil(math.log2(nc))))
    tw = t_len * S5_GROUP
    ns2 = 2 * S5_STATE
    grp = lambda *tail: pl.BlockSpec((1,) + tail, lambda g: (g,) + (0,) * len(tail))

    kf, p_op, g1, mul = pl.pallas_call(
        functools.partial(_s5_param_kernel, n_levels=n_levels),
        grid=(groups,),
        in_specs=[grp(1, S5_STATE), grp(1, S5_STATE), grp(1, 1),
                  grp(S5_GROUP, S5_STATE), grp(S5_GROUP, S5_STATE),
                  grp(S5_GROUP, S5_STATE), grp(S5_GROUP, S5_STATE)],
        out_specs=[grp(tw, S5_GROUP), grp(tw, ns2), grp(tw, ns2), grp(2 * n_levels, ns2)],
        out_shape=[jax.ShapeDtypeStruct((groups, tw, S5_GROUP), F32),
                   jax.ShapeDtypeStruct((groups, tw, ns2), F32),
                   jax.ShapeDtypeStruct((groups, tw, ns2), F32),
                   jax.ShapeDtypeStruct((groups, 2 * n_levels, ns2), F32)],
        compiler_params=_params(("arbitrary",)),
        name="s5_params",
    )(a_re.reshape(groups, 1, S5_STATE), a_im.reshape(groups, 1, S5_STATE), log_dt.reshape(groups, 1, 1),
      b_re.transpose(0, 2, 1), b_im.transpose(0, 2, 1), c_re, c_im)

    kf4 = jnp.concatenate([kf.reshape(groups, t_len, S5_GROUP, S5_GROUP),
                           jnp.zeros((groups, 1, S5_GROUP, S5_GROUP), F32)], axis=1)
    s_idx = jnp.arange(t_len)[:, None]
    t_idx = jnp.arange(t_len)[None, :]
    lag = jnp.where(t_idx >= s_idx, t_idx - s_idx, t_len)
    toep = kf4[:, lag].transpose(0, 1, 4, 2, 3).reshape(groups, tw, tw)
    w_op = jnp.concatenate([toep, p_op], axis=-1).astype(BF16)

    u_g = u.reshape(bsz, nc, t_len, groups, S5_GROUP).transpose(3, 0, 1, 2, 4).reshape(groups, bsz * nc, tw)
    d_row = jnp.tile(d_skip, (1, t_len)).reshape(groups, 1, tw)
    rows = bsz * nc
    y_g = pl.pallas_call(
        functools.partial(_s5_main_kernel, n_chunks=nc, n_levels=n_levels),
        grid=(groups,),
        in_specs=[grp(rows, tw), grp(tw, tw + ns2), grp(tw, ns2), grp(2 * n_levels, ns2), grp(1, tw)],
        out_specs=grp(rows, tw),
        out_shape=jax.ShapeDtypeStruct((groups, rows, tw), F32),
        compiler_params=_params(("arbitrary",)),
        name="s5_main",
    )(u_g, w_op, g1, mul, d_row)
    return y_g.reshape(groups, bsz, nc, t_len, S5_GROUP).transpose(1, 2, 3, 0, 4).reshape(bsz, seq, sw)


def _mix_out_kernel(x_ref, gt_ref, yg_ref, ys_ref, wglu_ref, wout_ref, npost_ref, o_ref):
    b = pl.program_id(0)
    gt = gt_ref[pl.ds(b, 1), :]
    ys = ys_ref[0]
    ys = ys * jax.nn.sigmoid(_dot(ys, wglu_ref[...]))
    gw = yg_ref.shape[2]
    y = _dot(yg_ref[0], wout_ref[:gw, :]) + _dot(ys, wout_ref[gw:, :])
    o_ref[0] = x_ref[0] + gt * _rms(y, npost_ref[...])


def _mix_out(x, mod, y_gdn, y_s5, w_glu, w_out, npost, tm):
    bsz, seq, d = x.shape
    gw = y_gdn.shape[2]
    sw = y_s5.shape[2]
    tok = lambda width: pl.BlockSpec((1, tm, width), lambda b, i: (b, i, 0))
    return pl.pallas_call(
        _mix_out_kernel,
        grid=(bsz, seq // tm),
        in_specs=[tok(d), _mod_specs(1, d)[2], tok(gw), tok(sw),
                  _resident((sw, sw), lambda b, i: (0, 0)),
                  _resident((gw + sw, d), lambda b, i: (0, 0)),
                  pl.BlockSpec((1, d), lambda b, i: (0, 0))],
        out_specs=tok(d),
        out_shape=jax.ShapeDtypeStruct(x.shape, F32),
        compiler_params=_params(("arbitrary", "arbitrary")),
        name="mix_out",
    )(x, mod, y_gdn, y_s5, w_glu.astype(BF16), w_out.astype(BF16), npost.reshape(1, d))


def kernel(x, c, w_mod, b_mod, ff1_norm_pre, ff1_norm_post, ff1_w_in, ff1_w_out, mix_norm_pre, mix_norm_post, mix_w_in, conv_w, a_log, dt_bias, gdn_norm_w, s5_a_re, s5_a_im, s5_log_dt, s5_b_re, s5_b_im, s5_c_re, s5_c_im, s5_d, s5_w_glu, mix_w_out, ff2_norm_pre, ff2_norm_post, ff2_w_in, ff2_w_out):
    depth = w_mod.shape[0]
    seq = x.shape[1]
    tm = min(512, seq)
    tl = min(256, seq)
    mods = _modulation(c, w_mod, b_mod)
    for l in range(depth):
        mod = mods[l]
        x = _ffn(x, mod, 0, ff1_norm_pre[l], ff1_norm_post[l], ff1_w_in[l], ff1_w_out[l], tm)
        q, k, v, z, u, bg = _mix_in(x, mod, mix_norm_pre[l], mix_w_in[l], conv_w[l], a_log[l], dt_bias[l], tm)
        y_gdn = _gdn(q, k, v, z, bg, gdn_norm_w[l], tl)
        y_s5 = _s5(u, s5_a_re[l], s5_a_im[l], s5_log_dt[l], s5_b_re[l], s5_b_im[l],
                   s5_c_re[l], s5_c_im[l], s5_d[l])
        x = _mix_out(x, mod, y_gdn, y_s5, s5_w_glu[l], mix_w_out[l], mix_norm_post[l], tm)
        x = _ffn(x, mod, 2, ff2_norm_pre[l], ff2_norm_post[l], ff2_w_in[l], ff2_w_out[l], tm)
    return x
```

```python
import functools
import math

import jax
import jax.numpy as jnp
from jax import lax
from jax.experimental import pallas as pl
from jax.experimental.pallas import tpu as pltpu

F32 = jnp.float32
BF16 = jnp.bfloat16
EPS = 1e-6

HEAD_DIM = 128
GDN_CHUNK = 64
CONV_K = 4
S5_GROUP = 16
S5_STATE = 64
N_MOD = 9
S5_T = 32
BG_LANES = 128

V7X_VMEM_LIMIT_BYTES = 56 * 1024 * 1024
HIGHEST = lax.Precision.HIGHEST


def _params(semantics):
    return pltpu.CompilerParams(dimension_semantics=semantics, vmem_limit_bytes=V7X_VMEM_LIMIT_BYTES)


def _resident(block_shape, index_map):
    return pl.BlockSpec(block_shape, index_map, pipeline_mode=pl.Buffered(1))


def _dot(a, b):
    return jnp.dot(a.astype(BF16), b.astype(BF16), preferred_element_type=F32)


def _dot_nt(a, b):
    return lax.dot_general(a.astype(BF16), b.astype(BF16), (((1,), (1,)), ((), ())),
                           preferred_element_type=F32)


def _dot_tn(a, b):
    return lax.dot_general(a.astype(BF16), b.astype(BF16), (((0,), (0,)), ((), ())),
                           preferred_element_type=F32)


def _dot_split(a, b):
    a_hi = a.astype(BF16)
    a_lo = (a - a_hi.astype(F32)).astype(BF16)
    b_hi = b.astype(BF16)
    b_lo = (b - b_hi.astype(F32)).astype(BF16)
    mm = functools.partial(jnp.dot, preferred_element_type=F32)
    return mm(a_hi, b_hi) + mm(a_hi, b_lo) + mm(a_lo, b_hi)


def _dot_f32(a, b):
    return jnp.dot(a, b, precision=HIGHEST, preferred_element_type=F32)


def _rms(x, w):
    return x * lax.rsqrt(jnp.mean(x * x, axis=-1, keepdims=True) + EPS) * w


def _silu(x):
    return x * jax.nn.sigmoid(x)


def _div_pow2(x, n):
    assert n & (n - 1) == 0
    return jnp.right_shift(x, n.bit_length() - 1)


def _mod_pow2(x, n):
    assert n & (n - 1) == 0
    return jnp.bitwise_and(x, n - 1)


def _mod_kernel(c_ref, w_ref, b_ref, o_ref):
    c = c_ref[...]
    o_ref[0] = _dot(_silu(c), w_ref[0]) + b_ref[0]


def _modulation(c, w_mod, b_mod):
    depth, d, n = w_mod.shape
    bsz = c.shape[0]
    rows = 8
    assert bsz <= rows
    tn = n // 8
    c_pad = jnp.pad(c, ((0, rows - bsz), (0, 0)))
    return pl.pallas_call(
        _mod_kernel,
        grid=(depth, n // tn),
        in_specs=[
            pl.BlockSpec((rows, d), lambda l, j: (0, 0)),
            pl.BlockSpec((1, d, tn), lambda l, j: (l, 0, j)),
            pl.BlockSpec((1, 1, tn), lambda l, j: (l, 0, j)),
        ],
        out_specs=pl.BlockSpec((1, rows, tn), lambda l, j: (l, 0, j)),
        out_shape=jax.ShapeDtypeStruct((depth, rows, n), F32),
        compiler_params=_params(("arbitrary", "arbitrary")),
        name="adaln_mod",
    )(c_pad, w_mod, b_mod.reshape(depth, 1, n))


def _mod_specs(sub, d):
    return [pl.BlockSpec((8, d), functools.partial(lambda b, i, j: (0, j), j=3 * sub + k)) for k in range(3)]


def _ffn_kernel(x_ref, sh_ref, sc_ref, gt_ref, npre_ref, npost_ref, win_ref, wout_ref, o_ref, acc_ref,
                *, n_chunks, tf):
    b = pl.program_id(0)
    x = x_ref[0]
    sh = sh_ref[pl.ds(b, 1), :]
    sc = sc_ref[pl.ds(b, 1), :]
    gt = gt_ref[pl.ds(b, 1), :]
    h = (_rms(x, npre_ref[...]) * (1.0 + sc) + sh).astype(BF16)
    f = wout_ref.shape[0]
    for ci in range(n_chunks):
        lo, hi = ci * tf, (ci + 1) * tf
        gate = jnp.dot(h, win_ref[:, lo:hi], preferred_element_type=F32)
        up = jnp.dot(h, win_ref[:, f + lo:f + hi], preferred_element_type=F32)
        part = _dot(_silu(gate) * up, wout_ref[lo:hi, :])
        if ci == 0:
            acc_ref[...] = part
        else:
            acc_ref[...] += part
    o_ref[0] = x + (0.5 * gt) * _rms(acc_ref[...], npost_ref[...])


def _ffn(x, mod, sub, npre, npost, w_in, w_out, tm):
    bsz, seq, d = x.shape
    f = w_out.shape[0]
    tf = 256
    assert f % tf == 0 and seq % tm == 0
    n = f // tf
    win = w_in.astype(BF16)
    wout = w_out.astype(BF16)
    return pl.pallas_call(
        functools.partial(_ffn_kernel, n_chunks=n, tf=tf),
        grid=(bsz, seq // tm),
        in_specs=[pl.BlockSpec((1, tm, d), lambda b, i: (b, i, 0))] + _mod_specs(sub, d) + [
            pl.BlockSpec((1, d), lambda b, i: (0, 0)),
            pl.BlockSpec((1, d), lambda b, i: (0, 0)),
            _resident((d, 2 * f), lambda b, i: (0, 0)),
            _resident((f, d), lambda b, i: (0, 0)),
        ],
        out_specs=pl.BlockSpec((1, tm, d), lambda b, i: (b, i, 0)),
        out_shape=jax.ShapeDtypeStruct(x.shape, F32),
        scratch_shapes=[pltpu.VMEM((tm, d), F32)],
        compiler_params=_params(("arbitrary", "arbitrary")),
        name="ffn",
    )(x, mod, mod, mod, npre.reshape(1, d), npost.reshape(1, d), win, wout)


def _mix_in_kernel(x_ref, sh_ref, sc_ref, npre_ref, w_ref, conv_ref, alog_ref, dtb_ref,
                   q_ref, k_ref, v_ref, z_ref, u_ref, bg_ref, ext_ref, *, heads, gw):
    b = pl.program_id(0)
    i = pl.program_id(1)
    tm = x_ref.shape[1]
    x = x_ref[0]
    sh = sh_ref[pl.ds(b, 1), :]
    sc = sc_ref[pl.ds(b, 1), :]
    h = (_rms(x, npre_ref[...]) * (1.0 + sc) + sh).astype(BF16)
    proj = lambda lo, hi: jnp.dot(h, w_ref[:, lo:hi], preferred_element_type=F32)

    @pl.when(i == 0)
    def _():
        ext_ref[0:8, :] = jnp.zeros((8, 3 * gw), F32)

    for blk, ref in enumerate((q_ref, k_ref, v_ref)):
        cols = slice(blk * gw, (blk + 1) * gw)
        ext_ref[8:, cols] = proj(blk * gw, (blk + 1) * gw)
        conv = conv_ref[0:1, cols] * ext_ref[pl.ds(8 - (CONV_K - 1), tm), cols]
        for j in range(1, CONV_K):
            conv = conv + conv_ref[j:j + 1, cols] * ext_ref[pl.ds(8 - (CONV_K - 1) + j, tm), cols]
        ext_ref[0:8, cols] = ext_ref[tm:tm + 8, cols]
        act = _silu(conv)
        if ref is v_ref:
            ref[0] = act
        else:
            for hd in range(heads):
                t = act[:, hd * HEAD_DIM:(hd + 1) * HEAD_DIM]
                ref[0, :, hd * HEAD_DIM:(hd + 1) * HEAD_DIM] = t * lax.rsqrt(
                    jnp.sum(t * t, axis=-1, keepdims=True) + EPS)
    z_ref[0] = proj(3 * gw, 4 * gw)
    sw = u_ref.shape[2]
    u_ref[0] = proj(4 * gw, 4 * gw + sw)

    ba = proj(4 * gw + sw, 4 * gw + sw + BG_LANES)
    beta = jax.nn.sigmoid(ba)
    t = ba + dtb_ref[...]
    softplus = jnp.maximum(t, 0.0) + jnp.log1p(jnp.exp(-jnp.abs(t)))
    g = -jnp.exp(alog_ref[...]) * softplus
    lane = lax.broadcasted_iota(jnp.int32, ba.shape, 1)
    bg_ref[0] = jnp.where(lane < heads, beta, g)


def _mix_in(x, mod, npre, w_in, conv_w, a_log, dt_bias, tm):
    bsz, seq, d = x.shape
    gw = d // 2
    sw = d - gw
    heads = gw // HEAD_DIM
    assert 2 * heads <= BG_LANES and seq % tm == 0
    qkv_w, z_w, beta_w, a_w, u_w = jnp.split(
        w_in, [3 * gw, 4 * gw, 4 * gw + heads, 4 * gw + 2 * heads], axis=1)
    pad = jnp.zeros((d, BG_LANES - 2 * heads), w_in.dtype)
    w = jnp.concatenate([qkv_w, z_w, u_w, beta_w, a_w, pad], axis=1).astype(BF16)
    n = w.shape[1]
    lane_pad = (0, BG_LANES - 2 * heads)
    alog = jnp.pad(jnp.concatenate([jnp.zeros_like(a_log), a_log]), lane_pad).reshape(1, BG_LANES)
    dtb = jnp.pad(jnp.concatenate([jnp.zeros_like(dt_bias), dt_bias]), lane_pad).reshape(1, BG_LANES)
    tok = lambda width: pl.BlockSpec((1, tm, width), lambda b, i: (b, i, 0))
    shp = lambda width: jax.ShapeDtypeStruct((bsz, seq, width), F32)
    return pl.pallas_call(
        functools.partial(_mix_in_kernel, heads=heads, gw=gw),
        grid=(bsz, seq // tm),
        in_specs=[tok(d)] + _mod_specs(1, d)[:2] + [
            pl.BlockSpec((1, d), lambda b, i: (0, 0)),
            _resident((d, n), lambda b, i: (0, 0)),
            pl.BlockSpec((CONV_K, 3 * gw), lambda b, i: (0, 0)),
            pl.BlockSpec((1, BG_LANES), lambda b, i: (0, 0)),
            pl.BlockSpec((1, BG_LANES), lambda b, i: (0, 0)),
        ],
        out_specs=[tok(gw), tok(gw), tok(gw), tok(gw), tok(sw), tok(BG_LANES)],
        out_shape=[shp(gw), shp(gw), shp(gw), shp(gw), shp(sw), shp(BG_LANES)],
        scratch_shapes=[pltpu.VMEM((tm + 8, 3 * gw), F32)],
        compiler_params=_params(("arbitrary", "arbitrary")),
        name="mix_in",
    )(x, mod, mod, npre.reshape(1, d), w, conv_w, alog, dtb)


def _gdn_kernel(q_ref, k_ref, v_ref, z_ref, bg_ref, nw_ref, y_ref, s_ref, *, heads, n_chunks):
    c = GDN_CHUNK

    @pl.when(pl.program_id(1) == 0)
    def _():
        s_ref[...] = jnp.zeros_like(s_ref)

    row = lax.broadcasted_iota(jnp.int32, (c, c), 0)
    col = lax.broadcasted_iota(jnp.int32, (c, c), 1)
    lower_incl = row >= col
    strict = row > col
    ltri = lower_incl.astype(F32)
    utri = (row <= col).astype(F32)
    eye = (row == col).astype(F32)
    ones_c = jnp.ones((c, c), F32)
    ones_hd = jnp.ones((HEAD_DIM, c), F32)
    scale = HEAD_DIM ** -0.5
    nw = nw_ref[...]

    pairs = [(ci, hd) for ci in range(n_chunks) for hd in range(heads)]
    rows_of = lambda ci: slice(ci * c, (ci + 1) * c)
    lanes_of = lambda hd: slice(hd * HEAD_DIM, (hd + 1) * HEAD_DIM)
    lane1 = lambda x, j: x[:, j:j + 1]

    bgs = [bg_ref[0, rows_of(ci), :] for ci in range(n_chunks)]
    gc_alls = [_dot_f32(ltri, bg) for bg in bgs]
    gl_alls = [_dot_f32(ones_hd, bg) for bg in bgs]
    beta = {p: lane1(bgs[p[0]], p[1]) for p in pairs}
    gc = {p: lane1(gc_alls[p[0]], heads + p[1]) for p in pairs}
    gl = {p: lane1(gl_alls[p[0]], heads + p[1]) for p in pairs}
    gc_row = {p: _dot_f32(ones_c, lane1(bgs[p[0]], heads + p[1]) * utri) for p in pairs}
    decay = {p: jnp.exp(jnp.where(lower_incl, gc[p] - gc_row[p], -jnp.inf)) for p in pairs}
    q = {p: q_ref[0, rows_of(p[0]), lanes_of(p[1])] * scale for p in pairs}
    k = {p: k_ref[0, rows_of(p[0]), lanes_of(p[1])] for p in pairs}
    kb = {p: k[p] * beta[p] for p in pairs}
    a_mat = {p: jnp.where(strict, -(_dot_nt(kb[p], k[p]) * decay[p]), 0.0) for p in pairs}
    a_pow = dict(a_mat)
    inv = {p: eye + a_mat[p] for p in pairs}
    for _ in range(int(math.log2(c)) - 1):
        a_pow = {p: _dot(a_pow[p], a_pow[p]) for p in pairs}
        inv = {p: inv[p] + _dot(inv[p], a_pow[p]) for p in pairs}
    resid = {p: (eye - inv[p]) + _dot_split(a_mat[p], inv[p]) for p in pairs}
    inv = {p: inv[p] + _dot(inv[p], resid[p]) for p in pairs}
    eg = {p: jnp.exp(gc[p]) for p in pairs}
    sol = {p: _dot(inv[p], jnp.concatenate(
        [v_ref[0, rows_of(p[0]), lanes_of(p[1])] * beta[p], kb[p] * eg[p]], axis=-1)) for p in pairs}
    attn = {p: _dot_nt(q[p], k[p]) * decay[p] for p in pairs}
    k_state = {p: k[p] * jnp.exp(gl[p][:c] - gc[p]) for p in pairs}
    q_state = {p: q[p] * eg[p] for p in pairs}

    state = [s_ref[hd] for hd in range(heads)]
    for ci in range(n_chunks):
        hp = [(ci, hd) for hd in range(heads)]
        sb = [state[hd].astype(BF16) for hd in range(heads)]
        v_new = [sol[p][:, :HEAD_DIM] - _dot(sol[p][:, HEAD_DIM:], sb[p[1]]) for p in hp]
        o = [_dot(q_state[p], sb[p[1]]) + _dot(attn[p], v_new[p[1]]) for p in hp]
        state = [state[p[1]] * jnp.exp(gl[p]) + _dot_tn(k_state[p], v_new[p[1]]) for p in hp]
        for p in hp:
            z = z_ref[0, rows_of(ci), lanes_of(p[1])]
            y_ref[0, rows_of(ci), lanes_of(p[1])] = _rms(o[p[1]], nw) * _silu(z)
    for hd in range(heads):
        s_ref[hd] = state[hd]


def _gdn(q, k, v, z, bg, norm_w, tl):
    bsz, seq, gw = q.shape
    heads = gw // HEAD_DIM
    assert seq % tl == 0 and tl % GDN_CHUNK == 0
    tok = lambda width: pl.BlockSpec((1, tl, width), lambda b, i: (b, i, 0))
    return pl.pallas_call(
        functools.partial(_gdn_kernel, heads=heads, n_chunks=tl // GDN_CHUNK),
        grid=(bsz, seq // tl),
        in_specs=[tok(gw), tok(gw), tok(gw), tok(gw), tok(BG_LANES),
                  pl.BlockSpec((1, HEAD_DIM), lambda b, i: (0, 0))],
        out_specs=tok(gw),
        out_shape=jax.ShapeDtypeStruct((bsz, seq, gw), F32),
        scratch_shapes=[pltpu.VMEM((heads, HEAD_DIM, HEAD_DIM), F32)],
        compiler_params=_params(("arbitrary", "arbitrary")),
        name="gdn",
    )(q, k, v, z, bg, norm_w.reshape(1, HEAD_DIM))


def _s5_param_kernel(are_ref, aim_ref, ldt_ref, brt_ref, bit_ref, cr_ref, ci_ref,
                     kf_ref, p_ref, g1_ref, mul_ref, *, n_levels):
    t_len = S5_T
    ns = S5_STATE
    hw = S5_GROUP
    ar = jnp.minimum(are_ref[0], -1e-4)
    ai = aim_ref[0]
    dt = jnp.exp(ldt_ref[0])
    mag = jnp.exp(dt * ar)
    abar_re = mag * jnp.cos(dt * ai)
    abar_im = mag * jnp.sin(dt * ai)
    denom = ar * ar + ai * ai
    zr = abar_re - 1.0
    zi = abar_im
    fr = (zr * ar + zi * ai) / denom
    fi = (zi * ar - zr * ai) / denom
    brt = brt_ref[0]
    bit = bit_ref[0]
    bbar_re_t = fr * brt - fi * bit
    bbar_im_t = fr * bit + fi * brt

    n_tab = t_len + 8
    tau = lax.broadcasted_iota(jnp.int32, (n_tab, ns), 0).astype(F32)
    tab_mag = jnp.exp(tau * (dt * ar))
    lam_re = tab_mag * jnp.cos(tau * (dt * ai))
    lam_im = tab_mag * jnp.sin(tau * (dt * ai))

    rows = t_len * hw
    r_tau = _div_pow2(lax.broadcasted_iota(jnp.int32, (rows, n_tab), 0), hw)
    c_tau = lax.broadcasted_iota(jnp.int32, (rows, n_tab), 1)
    rep0 = (c_tau == r_tau).astype(F32)
    rep1 = (c_tau == r_tau + 1).astype(F32)
    rep_rev = (c_tau == t_len - 1 - r_tau).astype(F32)
    r_h = _mod_pow2(lax.broadcasted_iota(jnp.int32, (rows, hw), 0), hw)
    c_h = lax.broadcasted_iota(jnp.int32, (rows, hw), 1)
    tile = (r_h == c_h).astype(F32)

    c_re_t = _dot_f32(tile, cr_ref[0])
    c_im_t = _dot_f32(tile, ci_ref[0])
    b_re_t = _dot_f32(tile, bbar_re_t)
    b_im_t = _dot_f32(tile, bbar_im_t)

    def c_lam(rep):
        lr = _dot_f32(rep, lam_re)
        li = _dot_f32(rep, lam_im)
        return jnp.concatenate([c_re_t * lr - c_im_t * li, -(c_re_t * li + c_im_t * lr)], axis=-1)

    bmat_t = jnp.concatenate([bbar_re_t, bbar_im_t], axis=-1)
    kf_ref[0] = lax.dot_general(c_lam(rep0), bmat_t, (((1,), (1,)), ((), ())),
                                precision=HIGHEST, preferred_element_type=F32)
    g1_ref[0] = c_lam(rep1)
    lr = _dot_f32(rep_rev, lam_re)
    li = _dot_f32(rep_rev, lam_im)
    p_ref[0] = jnp.concatenate([lr * b_re_t - li * b_im_t, lr * b_im_t + li * b_re_t], axis=-1)

    step_re = lam_re[t_len:t_len + 1, :]
    step_im = lam_im[t_len:t_len + 1, :]
    for lvl in range(n_levels):
        mul_ref[0, 2 * lvl:2 * lvl + 1, :] = jnp.concatenate([step_re, step_re], axis=-1)
        mul_ref[0, 2 * lvl + 1:2 * lvl + 2, :] = jnp.concatenate([-step_im, step_im], axis=-1)
        step_re, step_im = step_re * step_re - step_im * step_im, 2.0 * step_re * step_im


def _s5_main_kernel(u_ref, w_ref, g1_ref, mul_ref, d_ref, y_ref, *, n_chunks, n_levels):
    tw = u_ref.shape[2]
    u = u_ref[0]
    yz = _dot(u, w_ref[0])
    state = yz[:, tw:]
    chunk = _mod_pow2(lax.broadcasted_iota(jnp.int32, state.shape, 0), n_chunks)
    for lvl in range(n_levels):
        dist = 2 ** lvl
        prev = jnp.where(chunk >= dist, pltpu.roll(state, dist, axis=0), 0.0)
        state = (state + prev * mul_ref[0, 2 * lvl:2 * lvl + 1, :]
                 + pltpu.roll(prev, S5_STATE, axis=1) * mul_ref[0, 2 * lvl + 1:2 * lvl + 2, :])
    incoming = jnp.where(chunk >= 1, pltpu.roll(state, 1, axis=0), 0.0)
    y = yz[:, :tw] + _dot_nt(incoming, g1_ref[0]) + d_ref[0] * u
    y_ref[0] = jax.nn.gelu(y)


def _s5(u, a_re, a_im, log_dt, b_re, b_im, c_re, c_im, d_skip):
    bsz, seq, sw = u.shape
    groups = sw // S5_GROUP
    t_len = S5_T
    assert seq % t_len == 0
    nc = seq // t_len
    n_levels = max(1, int(math.ceil(math.log2(nc))))
    tw = t_len * S5_GROUP
    ns2 = 2 * S5_STATE
    grp = lambda *tail: pl.BlockSpec((1,) + tail, lambda g: (g,) + (0,) * len(tail))

    kf, p_op, g1, mul = pl.pallas_call(
        functools.partial(_s5_param_kernel, n_levels=n_levels),
        grid=(groups,),
        in_specs=[grp(1, S5_STATE), grp(1, S5_STATE), grp(1, 1),
                  grp(S5_GROUP, S5_STATE), grp(S5_GROUP, S5_STATE),
                  grp(S5_GROUP, S5_STATE), grp(S5_GROUP, S5_STATE)],
        out_specs=[grp(tw, S5_GROUP), grp(tw, ns2), grp(tw, ns2), grp(2 * n_levels, ns2)],
        out_shape=[jax.ShapeDtypeStruct((groups, tw, S5_GROUP), F32),
                   jax.ShapeDtypeStruct((groups, tw, ns2), F32),
                   jax.ShapeDtypeStruct((groups, tw, ns2), F32),
                   jax.ShapeDtypeStruct((groups, 2 * n_levels, ns2), F32)],
        compiler_params=_params(("arbitrary",)),
        name="s5_params",
    )(a_re.reshape(groups, 1, S5_STATE), a_im.reshape(groups, 1, S5_STATE), log_dt.reshape(groups, 1, 1),
      b_re.transpose(0, 2, 1), b_im.transpose(0, 2, 1), c_re, c_im)

    kf4 = jnp.concatenate([kf.reshape(groups, t_len, S5_GROUP, S5_GROUP),
                           jnp.zeros((groups, 1, S5_GROUP, S5_GROUP), F32)], axis=1)
    s_idx = jnp.arange(t_len)[:, None]
    t_idx = jnp.arange(t_len)[None, :]
    lag = jnp.where(t_idx >= s_idx, t_idx - s_idx, t_len)
    toep = kf4[:, lag].transpose(0, 1, 4, 2, 3).reshape(groups, tw, tw)
    w_op = jnp.concatenate([toep, p_op], axis=-1).astype(BF16)

    u_g = u.reshape(bsz, nc, t_len, groups, S5_GROUP).transpose(3, 0, 1, 2, 4).reshape(groups, bsz * nc, tw)
    d_row = jnp.tile(d_skip, (1, t_len)).reshape(groups, 1, tw)
    rows = bsz * nc
    y_g = pl.pallas_call(
        functools.partial(_s5_main_kernel, n_chunks=nc, n_levels=n_levels),
        grid=(groups,),
        in_specs=[grp(rows, tw), grp(tw, tw + ns2), grp(tw, ns2), grp(2 * n_levels, ns2), grp(1, tw)],
        out_specs=grp(rows, tw),
        out_shape=jax.ShapeDtypeStruct((groups, rows, tw), F32),
        compiler_params=_params(("arbitrary",)),
        name="s5_main",
    )(u_g, w_op, g1, mul, d_row)
    return y_g.reshape(groups, bsz, nc, t_len, S5_GROUP).transpose(1, 2, 3, 0, 4).reshape(bsz, seq, sw)


def _mix_out_kernel(x_ref, gt_ref, yg_ref, ys_ref, wglu_ref, wout_ref, npost_ref, o_ref):
    b = pl.program_id(0)
    gt = gt_ref[pl.ds(b, 1), :]
    ys = ys_ref[0]
    ys = ys * jax.nn.sigmoid(_dot(ys, wglu_ref[...]))
    gw = yg_ref.shape[2]
    y = _dot(yg_ref[0], wout_ref[:gw, :]) + _dot(ys, wout_ref[gw:, :])
    o_ref[0] = x_ref[0] + gt * _rms(y, npost_ref[...])


def _mix_out(x, mod, y_gdn, y_s5, w_glu, w_out, npost, tm):
    bsz, seq, d = x.shape
    gw = y_gdn.shape[2]
    sw = y_s5.shape[2]
    tok = lambda width: pl.BlockSpec((1, tm, width), lambda b, i: (b, i, 0))
    return pl.pallas_call(
        _mix_out_kernel,
        grid=(bsz, seq // tm),
        in_specs=[tok(d), _mod_specs(1, d)[2], tok(gw), tok(sw),
                  _resident((sw, sw), lambda b, i: (0, 0)),
                  _resident((gw + sw, d), lambda b, i: (0, 0)),
                  pl.BlockSpec((1, d), lambda b, i: (0, 0))],
        out_specs=tok(d),
        out_shape=jax.ShapeDtypeStruct(x.shape, F32),
        compiler_params=_params(("arbitrary", "arbitrary")),
        name="mix_out",
    )(x, mod, y_gdn, y_s5, w_glu.astype(BF16), w_out.astype(BF16), npost.reshape(1, d))


def kernel(x, c, w_mod, b_mod, ff1_norm_pre, ff1_norm_post, ff1_w_in, ff1_w_out, mix_norm_pre, mix_norm_post, mix_w_in, conv_w, a_log, dt_bias, gdn_norm_w, s5_a_re, s5_a_im, s5_log_dt, s5_b_re, s5_b_im, s5_c_re, s5_c_im, s5_d, s5_w_glu, mix_w_out, ff2_norm_pre, ff2_norm_post, ff2_w_in, ff2_w_out):
    depth = w_mod.shape[0]
    seq = x.shape[1]
    tm = min(512, seq)
    tl = min(256, seq)
    mods = _modulation(c, w_mod, b_mod)
    for l in range(depth):
        mod = mods[l]
        x = _ffn(x, mod, 0, ff1_norm_pre[l], ff1_norm_post[l], ff1_w_in[l], ff1_w_out[l], tm)
        q, k, v, z, u, bg = _mix_in(x, mod, mix_norm_pre[l], mix_w_in[l], conv_w[l], a_log[l], dt_bias[l], tm)
        y_gdn = _gdn(q, k, v, z, bg, gdn_norm_w[l], tl)
        y_s5 = _s5(u, s5_a_re[l], s5_a_im[l], s5_log_dt[l], s5_b_re[l], s5_b_im[l],
                   s5_c_re[l], s5_c_im[l], s5_d[l])
        x = _mix_out(x, mod, y_gdn, y_s5, s5_w_glu[l], mix_w_out[l], mix_norm_post[l], tm)
        x = _ffn(x, mod, 2, ff2_norm_pre[l], ff2_norm_post[l], ff2_w_in[l], ff2_w_out[l], tm)
    return x
```

```python
import functools
import math

import jax
import jax.numpy as jnp
from jax import lax
from jax.experimental import pallas as pl
from jax.experimental.pallas import tpu as pltpu

F32 = jnp.float32
BF16 = jnp.bfloat16
EPS = 1e-6

HEAD_DIM = 128
GDN_CHUNK = 64
CONV_K = 4
S5_GROUP = 16
S5_STATE = 64
N_MOD = 9
S5_T = 128
BG_LANES = 128

V7X_VMEM_LIMIT_BYTES = 56 * 1024 * 1024
HIGHEST = lax.Precision.HIGHEST


def _params(semantics):
    return pltpu.CompilerParams(dimension_semantics=semantics, vmem_limit_bytes=V7X_VMEM_LIMIT_BYTES)


def _resident(block_shape, index_map):
    return pl.BlockSpec(block_shape, index_map, pipeline_mode=pl.Buffered(1))


def _dot(a, b):
    return jnp.dot(a.astype(BF16), b.astype(BF16), preferred_element_type=F32)


def _dot_nt(a, b):
    return lax.dot_general(a.astype(BF16), b.astype(BF16), (((1,), (1,)), ((), ())),
                           preferred_element_type=F32)


def _dot_tn(a, b):
    return lax.dot_general(a.astype(BF16), b.astype(BF16), (((0,), (0,)), ((), ())),
                           preferred_element_type=F32)


def _dot_split(a, b):
    a_hi = a.astype(BF16)
    a_lo = (a - a_hi.astype(F32)).astype(BF16)
    b_hi = b.astype(BF16)
    b_lo = (b - b_hi.astype(F32)).astype(BF16)
    mm = functools.partial(jnp.dot, preferred_element_type=F32)
    return mm(a_hi, b_hi) + mm(a_hi, b_lo) + mm(a_lo, b_hi)


def _dot_f32(a, b):
    return jnp.dot(a, b, precision=HIGHEST, preferred_element_type=F32)


def _rms(x, w):
    return x * lax.rsqrt(jnp.mean(x * x, axis=-1, keepdims=True) + EPS) * w


def _silu(x):
    return x * jax.nn.sigmoid(x)


def _div_pow2(x, n):
    assert n & (n - 1) == 0
    return jnp.right_shift(x, n.bit_length() - 1)


def _mod_pow2(x, n):
    assert n & (n - 1) == 0
    return jnp.bitwise_and(x, n - 1)


def _mod_kernel(c_ref, w_ref, b_ref, o_ref):
    c = c_ref[...]
    o_ref[0] = _dot(_silu(c), w_ref[0]) + b_ref[0]


def _modulation(c, w_mod, b_mod):
    depth, d, n = w_mod.shape
    bsz = c.shape[0]
    rows = 8
    assert bsz <= rows
    tn = n // 8
    c_pad = jnp.pad(c, ((0, rows - bsz), (0, 0)))
    return pl.pallas_call(
        _mod_kernel,
        grid=(depth, n // tn),
        in_specs=[
            pl.BlockSpec((rows, d), lambda l, j: (0, 0)),
            pl.BlockSpec((1, d, tn), lambda l, j: (l, 0, j)),
            pl.BlockSpec((1, 1, tn), lambda l, j: (l, 0, j)),
        ],
        out_specs=pl.BlockSpec((1, rows, tn), lambda l, j: (l, 0, j)),
        out_shape=jax.ShapeDtypeStruct((depth, rows, n), F32),
        compiler_params=_params(("arbitrary", "arbitrary")),
        name="adaln_mod",
    )(c_pad, w_mod, b_mod.reshape(depth, 1, n))


def _mod_specs(sub, d):
    return [pl.BlockSpec((8, d), functools.partial(lambda b, i, j: (0, j), j=3 * sub + k)) for k in range(3)]


def _ffn_kernel(x_ref, sh_ref, sc_ref, gt_ref, npre_ref, npost_ref, win_ref, wout_ref, o_ref, acc_ref,
                *, n_chunks, tf):
    b = pl.program_id(0)
    x = x_ref[0]
    sh = sh_ref[pl.ds(b, 1), :]
    sc = sc_ref[pl.ds(b, 1), :]
    gt = gt_ref[pl.ds(b, 1), :]
    h = (_rms(x, npre_ref[...]) * (1.0 + sc) + sh).astype(BF16)
    f = wout_ref.shape[0]
    for ci in range(n_chunks):
        lo, hi = ci * tf, (ci + 1) * tf
        gate = jnp.dot(h, win_ref[:, lo:hi], preferred_element_type=F32)
        up = jnp.dot(h, win_ref[:, f + lo:f + hi], preferred_element_type=F32)
        part = _dot(_silu(gate) * up, wout_ref[lo:hi, :])
        if ci == 0:
            acc_ref[...] = part
        else:
            acc_ref[...] += part
    o_ref[0] = x + (0.5 * gt) * _rms(acc_ref[...], npost_ref[...])


def _ffn(x, mod, sub, npre, npost, w_in, w_out, tm):
    bsz, seq, d = x.shape
    f = w_out.shape[0]
    tf = 256
    assert f % tf == 0 and seq % tm == 0
    n = f // tf
    win = w_in.astype(BF16)
    wout = w_out.astype(BF16)
    return pl.pallas_call(
        functools.partial(_ffn_kernel, n_chunks=n, tf=tf),
        grid=(bsz, seq // tm),
        in_specs=[pl.BlockSpec((1, tm, d), lambda b, i: (b, i, 0))] + _mod_specs(sub, d) + [
            pl.BlockSpec((1, d), lambda b, i: (0, 0)),
            pl.BlockSpec((1, d), lambda b, i: (0, 0)),
            _resident((d, 2 * f), lambda b, i: (0, 0)),
            _resident((f, d), lambda b, i: (0, 0)),
        ],
        out_specs=pl.BlockSpec((1, tm, d), lambda b, i: (b, i, 0)),
        out_shape=jax.ShapeDtypeStruct(x.shape, F32),
        scratch_shapes=[pltpu.VMEM((tm, d), F32)],
        compiler_params=_params(("arbitrary", "arbitrary")),
        name="ffn",
    )(x, mod, mod, mod, npre.reshape(1, d), npost.reshape(1, d), win, wout)


def _mix_in_kernel(x_ref, sh_ref, sc_ref, npre_ref, w_ref, wut_ref, conv_ref, alog_ref, dtb_ref,
                   q_ref, k_ref, v_ref, z_ref, ut_ref, bg_ref, ext_ref, *, heads, gw):
    b = pl.program_id(0)
    i = pl.program_id(1)
    tm = x_ref.shape[1]
    x = x_ref[0]
    sh = sh_ref[pl.ds(b, 1), :]
    sc = sc_ref[pl.ds(b, 1), :]
    h = (_rms(x, npre_ref[...]) * (1.0 + sc) + sh).astype(BF16)
    proj = lambda lo, hi: jnp.dot(h, w_ref[:, lo:hi], preferred_element_type=F32)

    @pl.when(i == 0)
    def _():
        ext_ref[0:8, :] = jnp.zeros((8, 3 * gw), F32)

    for blk, ref in enumerate((q_ref, k_ref, v_ref)):
        cols = slice(blk * gw, (blk + 1) * gw)
        ext_ref[8:, cols] = proj(blk * gw, (blk + 1) * gw)
        conv = conv_ref[0:1, cols] * ext_ref[pl.ds(8 - (CONV_K - 1), tm), cols]
        for j in range(1, CONV_K):
            conv = conv + conv_ref[j:j + 1, cols] * ext_ref[pl.ds(8 - (CONV_K - 1) + j, tm), cols]
        ext_ref[0:8, cols] = ext_ref[tm:tm + 8, cols]
        act = _silu(conv)
        if ref is v_ref:
            ref[0] = act
        else:
            for hd in range(heads):
                t = act[:, hd * HEAD_DIM:(hd + 1) * HEAD_DIM]
                ref[0, :, hd * HEAD_DIM:(hd + 1) * HEAD_DIM] = t * lax.rsqrt(
                    jnp.sum(t * t, axis=-1, keepdims=True) + EPS)
    z_ref[0] = proj(3 * gw, 4 * gw)

    ut = lax.dot_general(wut_ref[...], h, (((1,), (1,)), ((), ())), preferred_element_type=F32)
    for grp in range(ut_ref.shape[0]):
        for cc in range(tm // S5_T):
            ut_ref[grp, 0, cc] = ut[grp * S5_GROUP:(grp + 1) * S5_GROUP, cc * S5_T:(cc + 1) * S5_T]

    ba = proj(4 * gw, 4 * gw + BG_LANES)
    beta = jax.nn.sigmoid(ba)
    t = ba + dtb_ref[...]
    softplus = jnp.maximum(t, 0.0) + jnp.log1p(jnp.exp(-jnp.abs(t)))
    g = -jnp.exp(alog_ref[...]) * softplus
    lane = lax.broadcasted_iota(jnp.int32, ba.shape, 1)
    bg_ref[0] = jnp.where(lane < heads, beta, g)


def _mix_in(x, mod, npre, w_in, conv_w, a_log, dt_bias, tm):
    bsz, seq, d = x.shape
    gw = d // 2
    sw = d - gw
    heads = gw // HEAD_DIM
    groups = sw // S5_GROUP
    assert 2 * heads <= BG_LANES and seq % tm == 0 and tm % S5_T == 0
    nc = seq // S5_T
    qkv_w, z_w, beta_w, a_w, u_w = jnp.split(
        w_in, [3 * gw, 4 * gw, 4 * gw + heads, 4 * gw + 2 * heads], axis=1)
    pad = jnp.zeros((d, BG_LANES - 2 * heads), w_in.dtype)
    w = jnp.concatenate([qkv_w, z_w, beta_w, a_w, pad], axis=1).astype(BF16)
    wut = u_w.T.astype(BF16)
    n = w.shape[1]
    lane_pad = (0, BG_LANES - 2 * heads)
    alog = jnp.pad(jnp.concatenate([jnp.zeros_like(a_log), a_log]), lane_pad).reshape(1, BG_LANES)
    dtb = jnp.pad(jnp.concatenate([jnp.zeros_like(dt_bias), dt_bias]), lane_pad).reshape(1, BG_LANES)
    tok = lambda width: pl.BlockSpec((1, tm, width), lambda b, i: (b, i, 0))
    shp = lambda width: jax.ShapeDtypeStruct((bsz, seq, width), F32)
    return pl.pallas_call(
        functools.partial(_mix_in_kernel, heads=heads, gw=gw),
        grid=(bsz, seq // tm),
        in_specs=[tok(d)] + _mod_specs(1, d)[:2] + [
            pl.BlockSpec((1, d), lambda b, i: (0, 0)),
            _resident((d, n), lambda b, i: (0, 0)),
            _resident((sw, d), lambda b, i: (0, 0)),
            pl.BlockSpec((CONV_K, 3 * gw), lambda b, i: (0, 0)),
            pl.BlockSpec((1, BG_LANES), lambda b, i: (0, 0)),
            pl.BlockSpec((1, BG_LANES), lambda b, i: (0, 0)),
        ],
        out_specs=[tok(gw), tok(gw), tok(gw), tok(gw),
                   pl.BlockSpec((groups, 1, tm // S5_T, S5_GROUP, S5_T), lambda b, i: (0, b, i, 0, 0)),
                   tok(BG_LANES)],
        out_shape=[shp(gw), shp(gw), shp(gw), shp(gw),
                   jax.ShapeDtypeStruct((groups, bsz, nc, S5_GROUP, S5_T), F32),
                   shp(BG_LANES)],
        scratch_shapes=[pltpu.VMEM((tm + 8, 3 * gw), F32)],
        compiler_params=_params(("arbitrary", "arbitrary")),
        name="mix_in",
    )(x, mod, mod, npre.reshape(1, d), w, wut, conv_w, alog, dtb)


def _gdn_kernel(q_ref, k_ref, v_ref, z_ref, bg_ref, nw_ref, y_ref, s_ref, *, heads, n_chunks):
    c = GDN_CHUNK

    @pl.when(pl.program_id(1) == 0)
    def _():
        s_ref[...] = jnp.zeros_like(s_ref)

    row = lax.broadcasted_iota(jnp.int32, (c, c), 0)
    col = lax.broadcasted_iota(jnp.int32, (c, c), 1)
    lower_incl = row >= col
    strict = row > col
    ltri = lower_incl.astype(F32)
    utri = (row <= col).astype(F32)
    eye = (row == col).astype(F32)
    ones_c = jnp.ones((c, c), F32)
    ones_hd = jnp.ones((HEAD_DIM, c), F32)
    scale = HEAD_DIM ** -0.5
    nw = nw_ref[...]

    pairs = [(ci, hd) for ci in range(n_chunks) for hd in range(heads)]
    rows_of = lambda ci: slice(ci * c, (ci + 1) * c)
    lanes_of = lambda hd: slice(hd * HEAD_DIM, (hd + 1) * HEAD_DIM)
    lane1 = lambda x, j: x[:, j:j + 1]

    bgs = [bg_ref[0, rows_of(ci), :] for ci in range(n_chunks)]
    gc_alls = [_dot_f32(ltri, bg) for bg in bgs]
    gl_alls = [_dot_f32(ones_hd, bg) for bg in bgs]
    beta = {p: lane1(bgs[p[0]], p[1]) for p in pairs}
    gc = {p: lane1(gc_alls[p[0]], heads + p[1]) for p in pairs}
    gl = {p: lane1(gl_alls[p[0]], heads + p[1]) for p in pairs}
    gc_row = {p: _dot_f32(ones_c, lane1(bgs[p[0]], heads + p[1]) * utri) for p in pairs}
    decay = {p: jnp.exp(jnp.where(lower_incl, gc[p] - gc_row[p], -jnp.inf)) for p in pairs}
    q = {p: q_ref[0, rows_of(p[0]), lanes_of(p[1])] * scale for p in pairs}
    k = {p: k_ref[0, rows_of(p[0]), lanes_of(p[1])] for p in pairs}
    kb = {p: k[p] * beta[p] for p in pairs}
    a_mat = {p: jnp.where(strict, -(_dot_nt(kb[p], k[p]) * decay[p]), 0.0) for p in pairs}
    a_pow = dict(a_mat)
    inv = {p: eye + a_mat[p] for p in pairs}
    for _ in range(int(math.log2(c)) - 1):
        a_pow = {p: _dot(a_pow[p], a_pow[p]) for p in pairs}
        inv = {p: inv[p] + _dot(inv[p], a_pow[p]) for p in pairs}
    resid = {p: (eye - inv[p]) + _dot_split(a_mat[p], inv[p]) for p in pairs}
    inv = {p: inv[p] + _dot(inv[p], resid[p]) for p in pairs}
    eg = {p: jnp.exp(gc[p]) for p in pairs}
    sol = {p: _dot(inv[p], jnp.concatenate(
        [v_ref[0, rows_of(p[0]), lanes_of(p[1])] * beta[p], kb[p] * eg[p]], axis=-1)) for p in pairs}
    attn = {p: _dot_nt(q[p], k[p]) * decay[p] for p in pairs}
    k_state = {p: k[p] * jnp.exp(gl[p][:c] - gc[p]) for p in pairs}
    q_state = {p: q[p] * eg[p] for p in pairs}

    state = [s_ref[hd] for hd in range(heads)]
    for ci in range(n_chunks):
        hp = [(ci, hd) for hd in range(heads)]
        sb = [state[hd].astype(BF16) for hd in range(heads)]
        v_new = [sol[p][:, :HEAD_DIM] - _dot(sol[p][:, HEAD_DIM:], sb[p[1]]) for p in hp]
        o = [_dot(q_state[p], sb[p[1]]) + _dot(attn[p], v_new[p[1]]) for p in hp]
        state = [state[p[1]] * jnp.exp(gl[p]) + _dot_tn(k_state[p], v_new[p[1]]) for p in hp]
        for p in hp:
            z = z_ref[0, rows_of(ci), lanes_of(p[1])]
            y_ref[0, rows_of(ci), lanes_of(p[1])] = _rms(o[p[1]], nw) * _silu(z)
    for hd in range(heads):
        s_ref[hd] = state[hd]


def _gdn(q, k, v, z, bg, norm_w, tl):
    bsz, seq, gw = q.shape
    heads = gw // HEAD_DIM
    assert seq % tl == 0 and tl % GDN_CHUNK == 0
    tok = lambda width: pl.BlockSpec((1, tl, width), lambda b, i: (b, i, 0))
    return pl.pallas_call(
        functools.partial(_gdn_kernel, heads=heads, n_chunks=tl // GDN_CHUNK),
        grid=(bsz, seq // tl),
        in_specs=[tok(gw), tok(gw), tok(gw), tok(gw), tok(BG_LANES),
                  pl.BlockSpec((1, HEAD_DIM), lambda b, i: (0, 0))],
        out_specs=tok(gw),
        out_shape=jax.ShapeDtypeStruct((bsz, seq, gw), F32),
        scratch_shapes=[pltpu.VMEM((heads, HEAD_DIM, HEAD_DIM), F32)],
        compiler_params=_params(("arbitrary", "arbitrary")),
        name="gdn",
    )(q, k, v, z, bg, norm_w.reshape(1, HEAD_DIM))


def _s5_param_kernel(are_ref, aim_ref, arc_ref, aic_ref, ldt_ref, brt_ref, bit_ref, cr_ref, ci_ref,
                     crt_ref, cit_ref, kft_ref, p_ref, g1t_ref, mul_ref, *, n_levels):
    t_len = S5_T
    ns = S5_STATE
    hw = S5_GROUP
    dt = jnp.exp(ldt_ref[0])

    ar = jnp.minimum(are_ref[0], -1e-4)
    ai = aim_ref[0]
    mag = jnp.exp(dt * ar)
    abar_re = mag * jnp.cos(dt * ai)
    abar_im = mag * jnp.sin(dt * ai)
    denom = ar * ar + ai * ai
    zr = abar_re - 1.0
    zi = abar_im
    fr = (zr * ar + zi * ai) / denom
    fi = (zi * ar - zr * ai) / denom
    brt = brt_ref[0]
    bit = bit_ref[0]
    bbar_re_t = fr * brt - fi * bit
    bbar_im_t = fr * bit + fi * brt

    tau_rev = (t_len - 1 - lax.broadcasted_iota(jnp.int32, (t_len, ns), 0)).astype(F32)
    rev_mag = jnp.exp(tau_rev * (dt * ar))
    rev_re = rev_mag * jnp.cos(tau_rev * (dt * ai))
    rev_im = rev_mag * jnp.sin(tau_rev * (dt * ai))
    for hp in range(hw):
        b_r = bbar_re_t[hp:hp + 1, :]
        b_i = bbar_im_t[hp:hp + 1, :]
        p_ref[0, hp * t_len:(hp + 1) * t_len, :] = jnp.concatenate(
            [rev_re * b_r - rev_im * b_i, rev_re * b_i + rev_im * b_r], axis=-1).astype(BF16)

    ar_c = jnp.minimum(arc_ref[0], -1e-4)
    ai_c = aic_ref[0]
    tau = lax.broadcasted_iota(jnp.int32, (ns, t_len), 1).astype(F32)

    def lam_pow(t):
        m = jnp.exp(t * (dt * ar_c))
        return m * jnp.cos(t * (dt * ai_c)), m * jnp.sin(t * (dt * ai_c))

    pow_re, pow_im = lam_pow(tau)
    rows = hw * hw
    r_idx = lax.broadcasted_iota(jnp.int32, (rows, hw), 0)
    c_idx = lax.broadcasted_iota(jnp.int32, (rows, hw), 1)
    pick_h = (_mod_pow2(r_idx, hw) == c_idx).astype(F32)
    pick_hp = (_div_pow2(r_idx, hw) == c_idx).astype(F32)
    c_r = _dot_f32(pick_h, cr_ref[0])
    c_i = _dot_f32(pick_h, ci_ref[0])
    b_r = _dot_f32(pick_hp, bbar_re_t)
    b_i = _dot_f32(pick_hp, bbar_im_t)
    cb = jnp.concatenate([c_r * b_r - c_i * b_i, -(c_r * b_i + c_i * b_r)], axis=-1)
    kft_ref[0] = _dot_f32(cb, jnp.concatenate([pow_re, pow_im], axis=0))

    nxt_re, nxt_im = lam_pow(tau + 1.0)
    crt = crt_ref[0]
    cit = cit_ref[0]
    for h in range(hw):
        c_r = crt[:, h:h + 1]
        c_i = cit[:, h:h + 1]
        g1t_ref[0, :, h * t_len:(h + 1) * t_len] = jnp.concatenate(
            [c_r * nxt_re - c_i * nxt_im, -(c_r * nxt_im + c_i * nxt_re)], axis=0)

    step_mag = jnp.exp(float(t_len) * (dt * ar))
    step_re = step_mag * jnp.cos(float(t_len) * (dt * ai))
    step_im = step_mag * jnp.sin(float(t_len) * (dt * ai))
    for lvl in range(n_levels):
        mul_ref[0, 2 * lvl:2 * lvl + 1, :] = jnp.concatenate([step_re, step_re], axis=-1)
        mul_ref[0, 2 * lvl + 1:2 * lvl + 2, :] = jnp.concatenate([-step_im, step_im], axis=-1)
        step_re, step_im = step_re * step_re - step_im * step_im, 2.0 * step_re * step_im


def _s5_main_kernel(ut_ref, kft_ref, p_ref, g1t_ref, mul_ref, d_ref, yt_ref, m_ref, a_ref,
                    *, n_chunks, n_levels):
    t_len = S5_T
    hw = S5_GROUP
    rows = a_ref.shape[0]

    causal = (lax.broadcasted_iota(jnp.int32, (t_len, t_len), 1)
              >= lax.broadcasted_iota(jnp.int32, (t_len, t_len), 0))
    def toeplitz_rows(hp, carry):
        r0 = pl.multiple_of(hp * t_len, t_len)
        for h in range(hw):
            k_row = kft_ref[0, pl.ds(hp * hw + h, 1), :]
            blk = pltpu.roll(jnp.broadcast_to(k_row, (t_len, t_len)), 0, 1, stride=1, stride_axis=0)
            m_ref[pl.ds(r0, t_len), h * t_len:(h + 1) * t_len] = jnp.where(causal, blk, 0.0).astype(BF16)
        return carry

    lax.fori_loop(0, hw, toeplitz_rows, 0)

    for hp in range(hw):
        a_ref[:, hp * t_len:(hp + 1) * t_len] = ut_ref[0, pl.ds(hp, rows, stride=hw), :]
    a = a_ref[...]
    a_bf = a.astype(BF16)
    state = jnp.dot(a_bf, p_ref[0], preferred_element_type=F32)
    chunk = _mod_pow2(lax.broadcasted_iota(jnp.int32, state.shape, 0), n_chunks)
    for lvl in range(n_levels):
        dist = 2 ** lvl
        prev = jnp.where(chunk >= dist, pltpu.roll(state, dist, axis=0), 0.0)
        state = (state + prev * mul_ref[0, 2 * lvl:2 * lvl + 1, :]
                 + pltpu.roll(prev, S5_STATE, axis=1) * mul_ref[0, 2 * lvl + 1:2 * lvl + 2, :])
    incoming = jnp.where(chunk >= 1, pltpu.roll(state, 1, axis=0), 0.0)
    y = (jnp.dot(a_bf, m_ref[...], preferred_element_type=F32)
         + _dot(incoming, g1t_ref[0]) + d_ref[0] * a)
    y = jax.nn.gelu(y)
    for h in range(hw):
        yt_ref[0, pl.ds(h, rows, stride=hw), :] = y[:, h * t_len:(h + 1) * t_len]


def _s5(ut, a_re, a_im, log_dt, b_re, b_im, c_re, c_im, d_skip):
    groups, bsz, nc, hw, t_len = ut.shape
    n_levels = max(1, int(math.ceil(math.log2(nc))))
    tw = t_len * hw
    ns = S5_STATE
    grp = lambda *tail: pl.BlockSpec((1,) + tail, lambda g: (g,) + (0,) * len(tail))
    col = lambda a: a.reshape(groups, ns, 1)
    row = lambda a: a.reshape(groups, 1, ns)

    kft, p_op, g1t, mul = pl.pallas_call(
        functools.partial(_s5_param_kernel, n_levels=n_levels),
        grid=(groups,),
        in_specs=[grp(1, ns), grp(1, ns), grp(ns, 1), grp(ns, 1), grp(1, 1),
                  grp(hw, ns), grp(hw, ns), grp(hw, ns), grp(hw, ns), grp(ns, hw), grp(ns, hw)],
        out_specs=[grp(hw * hw, t_len), grp(tw, 2 * ns), grp(2 * ns, tw), grp(2 * n_levels, 2 * ns)],
        out_shape=[jax.ShapeDtypeStruct((groups, hw * hw, t_len), F32),
                   jax.ShapeDtypeStruct((groups, tw, 2 * ns), BF16),
                   jax.ShapeDtypeStruct((groups, 2 * ns, tw), F32),
                   jax.ShapeDtypeStruct((groups, 2 * n_levels, 2 * ns), F32)],
        compiler_params=_params(("arbitrary",)),
        name="s5_params",
    )(row(a_re), row(a_im), col(a_re), col(a_im), log_dt.reshape(groups, 1, 1),
      b_re.transpose(0, 2, 1), b_im.transpose(0, 2, 1), c_re, c_im,
      c_re.transpose(0, 2, 1), c_im.transpose(0, 2, 1))

    rows = bsz * nc
    d_row = jnp.repeat(d_skip, t_len, axis=1).reshape(groups, 1, tw)
    yt = pl.pallas_call(
        functools.partial(_s5_main_kernel, n_chunks=nc, n_levels=n_levels),
        grid=(groups,),
        in_specs=[grp(rows * hw, t_len), grp(hw * hw, t_len), grp(tw, 2 * ns), grp(2 * ns, tw),
                  grp(2 * n_levels, 2 * ns), grp(1, tw)],
        out_specs=grp(rows * hw, t_len),
        out_shape=jax.ShapeDtypeStruct((groups, rows * hw, t_len), F32),
        scratch_shapes=[pltpu.VMEM((tw, tw), BF16), pltpu.VMEM((rows, tw), F32)],
        compiler_params=_params(("arbitrary",)),
        name="s5_main",
    )(ut.reshape(groups, rows * hw, t_len), kft, p_op, g1t, mul, d_row)
    return yt.reshape(groups, bsz, nc, hw, t_len)


def _mix_out_kernel(x_ref, gt_ref, yg_ref, yst_ref, wglut_ref, wout_ref, npost_ref, o_ref, yt_ref):
    b = pl.program_id(0)
    gt = gt_ref[pl.ds(b, 1), :]
    for grp in range(yst_ref.shape[0]):
        for cc in range(yst_ref.shape[2]):
            yt_ref[grp * S5_GROUP:(grp + 1) * S5_GROUP, cc * S5_T:(cc + 1) * S5_T] = yst_ref[grp, 0, cc]
    yt = yt_ref[...]
    st = yt * jax.nn.sigmoid(_dot(wglut_ref[...], yt))
    gw = yg_ref.shape[2]
    y = _dot(yg_ref[0], wout_ref[:gw, :]) + _dot_tn(st, wout_ref[gw:, :])
    o_ref[0] = x_ref[0] + gt * _rms(y, npost_ref[...])


def _mix_out(x, mod, y_gdn, yt_s5, w_glu, w_out, npost, tm):
    bsz, seq, d = x.shape
    gw = y_gdn.shape[2]
    groups, _, _, hw, t_len = yt_s5.shape
    sw = groups * hw
    tok = lambda width: pl.BlockSpec((1, tm, width), lambda b, i: (b, i, 0))
    return pl.pallas_call(
        _mix_out_kernel,
        grid=(bsz, seq // tm),
        in_specs=[tok(d), _mod_specs(1, d)[2], tok(gw),
                  pl.BlockSpec((groups, 1, tm // t_len, hw, t_len), lambda b, i: (0, b, i, 0, 0)),
                  _resident((sw, sw), lambda b, i: (0, 0)),
                  _resident((gw + sw, d), lambda b, i: (0, 0)),
                  pl.BlockSpec((1, d), lambda b, i: (0, 0))],
        out_specs=tok(d),
        out_shape=jax.ShapeDtypeStruct(x.shape, F32),
        scratch_shapes=[pltpu.VMEM((sw, tm), F32)],
        compiler_params=_params(("arbitrary", "arbitrary")),
        name="mix_out",
    )(x, mod, y_gdn, yt_s5, w_glu.T.astype(BF16), w_out.astype(BF16), npost.reshape(1, d))


def kernel(x, c, w_mod, b_mod, ff1_norm_pre, ff1_norm_post, ff1_w_in, ff1_w_out, mix_norm_pre, mix_norm_post, mix_w_in, conv_w, a_log, dt_bias, gdn_norm_w, s5_a_re, s5_a_im, s5_log_dt, s5_b_re, s5_b_im, s5_c_re, s5_c_im, s5_d, s5_w_glu, mix_w_out, ff2_norm_pre, ff2_norm_post, ff2_w_in, ff2_w_out):
    depth = w_mod.shape[0]
    seq = x.shape[1]
    tm = min(512, seq)
    tl = min(256, seq)
    mods = _modulation(c, w_mod, b_mod)
    for l in range(depth):
        mod = mods[l]
        x = _ffn(x, mod, 0, ff1_norm_pre[l], ff1_norm_post[l], ff1_w_in[l], ff1_w_out[l], tm)
        q, k, v, z, ut, bg = _mix_in(x, mod, mix_norm_pre[l], mix_w_in[l], conv_w[l], a_log[l], dt_bias[l], tm)
        y_gdn = _gdn(q, k, v, z, bg, gdn_norm_w[l], tl)
        yt_s5 = _s5(ut, s5_a_re[l], s5_a_im[l], s5_log_dt[l], s5_b_re[l], s5_b_im[l],
                    s5_c_re[l], s5_c_im[l], s5_d[l])
        x = _mix_out(x, mod, y_gdn, yt_s5, s5_w_glu[l], mix_w_out[l], mix_norm_post[l], tm)
        x = _ffn(x, mod, 2, ff2_norm_pre[l], ff2_norm_post[l], ff2_w_in[l], ff2_w_out[l], tm)
    return x
```

```python
import functools
import math

import jax
import jax.numpy as jnp
from jax import lax
from jax.experimental import pallas as pl
from jax.experimental.pallas import tpu as pltpu

F32 = jnp.float32
BF16 = jnp.bfloat16
EPS = 1e-6

HEAD_DIM = 128
GDN_CHUNK = 64
CONV_K = 4
S5_GROUP = 16
S5_STATE = 64
N_MOD = 9
S5_T = 128
BG_LANES = 128

V7X_VMEM_LIMIT_BYTES = 56 * 1024 * 1024
HIGHEST = lax.Precision.HIGHEST


def _params(semantics):
    return pltpu.CompilerParams(dimension_semantics=semantics, vmem_limit_bytes=V7X_VMEM_LIMIT_BYTES)


def _resident(block_shape, index_map):
    return pl.BlockSpec(block_shape, index_map, pipeline_mode=pl.Buffered(1))


def _dot(a, b):
    return jnp.dot(a.astype(BF16), b.astype(BF16), preferred_element_type=F32)


def _dot_nt(a, b):
    return lax.dot_general(a.astype(BF16), b.astype(BF16), (((1,), (1,)), ((), ())),
                           preferred_element_type=F32)


def _dot_tn(a, b):
    return lax.dot_general(a.astype(BF16), b.astype(BF16), (((0,), (0,)), ((), ())),
                           preferred_element_type=F32)


def _dot_split(a, b):
    a_hi = a.astype(BF16)
    a_lo = (a - a_hi.astype(F32)).astype(BF16)
    b_hi = b.astype(BF16)
    b_lo = (b - b_hi.astype(F32)).astype(BF16)
    mm = functools.partial(jnp.dot, preferred_element_type=F32)
    return mm(a_hi, b_hi) + mm(a_hi, b_lo) + mm(a_lo, b_hi)


def _dot_01(sel, x):
    hi = x.astype(BF16)
    rest = x - hi.astype(F32)
    mid = rest.astype(BF16)
    lo = (rest - mid.astype(F32)).astype(BF16)
    mm = functools.partial(jnp.dot, preferred_element_type=F32)
    return mm(sel, hi) + mm(sel, mid) + mm(sel, lo)


def _dot_f32(a, b):
    return jnp.dot(a, b, precision=HIGHEST, preferred_element_type=F32)


def _rms(x, w):
    return x * lax.rsqrt(jnp.mean(x * x, axis=-1, keepdims=True) + EPS) * w


def _silu(x):
    return x * jax.nn.sigmoid(x)


def _div_pow2(x, n):
    assert n & (n - 1) == 0
    return jnp.right_shift(x, n.bit_length() - 1)


def _mod_pow2(x, n):
    assert n & (n - 1) == 0
    return jnp.bitwise_and(x, n - 1)


def _mod_kernel(c_ref, w_ref, b_ref, o_ref):
    c = c_ref[...]
    o_ref[0] = _dot(_silu(c), w_ref[0]) + b_ref[0]


def _modulation(c, w_mod, b_mod):
    depth, d, n = w_mod.shape
    bsz = c.shape[0]
    rows = 8
    assert bsz <= rows
    tn = n // 8
    c_pad = jnp.pad(c, ((0, rows - bsz), (0, 0)))
    return pl.pallas_call(
        _mod_kernel,
        grid=(depth, n // tn),
        in_specs=[
            pl.BlockSpec((rows, d), lambda l, j: (0, 0)),
            pl.BlockSpec((1, d, tn), lambda l, j: (l, 0, j)),
            pl.BlockSpec((1, 1, tn), lambda l, j: (l, 0, j)),
        ],
        out_specs=pl.BlockSpec((1, rows, tn), lambda l, j: (l, 0, j)),
        out_shape=jax.ShapeDtypeStruct((depth, rows, n), F32),
        compiler_params=_params(("arbitrary", "arbitrary")),
        name="adaln_mod",
    )(c_pad, w_mod, b_mod.reshape(depth, 1, n))


def _mod_specs(sub, d):
    return [pl.BlockSpec((8, d), functools.partial(lambda b, i, j: (0, j), j=3 * sub + k)) for k in range(3)]


def _ffn_kernel(x_ref, sh_ref, sc_ref, gt_ref, npre_ref, npost_ref, win_ref, wout_ref, o_ref, acc_ref,
                *, n_chunks, tf):
    b = pl.program_id(0)
    sh = sh_ref[pl.ds(b, 1), :]
    sc = sc_ref[pl.ds(b, 1), :]
    gt = gt_ref[pl.ds(b, 1), :]
    f = wout_ref.shape[0]
    n_sub, sub = acc_ref.shape[0], acc_ref.shape[1]
    rows = lambda s: slice(s * sub, (s + 1) * sub)

    def prologue(s):
        return (_rms(x_ref[0, rows(s), :], npre_ref[...]) * (1.0 + sc) + sh).astype(BF16)

    def chunk(s, h, ci):
        lo, hi = ci * tf, (ci + 1) * tf
        gate = jnp.dot(h, win_ref[:, lo:hi], preferred_element_type=F32)
        up = jnp.dot(h, win_ref[:, f + lo:f + hi], preferred_element_type=F32)
        part = _dot(_silu(gate) * up, wout_ref[lo:hi, :])
        if ci == 0:
            acc_ref[s] = part
        else:
            acc_ref[s] += part

    def epilogue(s):
        o_ref[0, rows(s), :] = x_ref[0, rows(s), :] + (0.5 * gt) * _rms(acc_ref[s], npost_ref[...])

    lead = 2
    h_cur = prologue(0)
    for s in range(n_sub):
        h_next = None
        for ci in range(n_chunks):
            chunk(s, h_cur, ci)
            if ci == n_chunks - 1 - lead and s + 1 < n_sub:
                h_next = prologue(s + 1)
            if ci == lead - 1 and s > 0:
                epilogue(s - 1)
        h_cur = h_next
    epilogue(n_sub - 1)


def _ffn(x, mod, sub, npre, npost, w_in, w_out, tm):
    bsz, seq, d = x.shape
    f = w_out.shape[0]
    tf = 256
    sub_rows = min(tm, 512)
    assert f % tf == 0 and seq % tm == 0 and tm % sub_rows == 0
    n = f // tf
    win = w_in.astype(BF16)
    wout = w_out.astype(BF16)
    return pl.pallas_call(
        functools.partial(_ffn_kernel, n_chunks=n, tf=tf),
        grid=(bsz, seq // tm),
        in_specs=[pl.BlockSpec((1, tm, d), lambda b, i: (b, i, 0))] + _mod_specs(sub, d) + [
            pl.BlockSpec((1, d), lambda b, i: (0, 0)),
            pl.BlockSpec((1, d), lambda b, i: (0, 0)),
            _resident((d, 2 * f), lambda b, i: (0, 0)),
            _resident((f, d), lambda b, i: (0, 0)),
        ],
        out_specs=pl.BlockSpec((1, tm, d), lambda b, i: (b, i, 0)),
        out_shape=jax.ShapeDtypeStruct(x.shape, F32),
        scratch_shapes=[pltpu.VMEM((tm // sub_rows, sub_rows, d), F32)],
        compiler_params=_params(("arbitrary", "arbitrary")),
        name="ffn",
    )(x, mod, mod, mod, npre.reshape(1, d), npost.reshape(1, d), win, wout)


def _mix_in_kernel(x_ref, sh_ref, sc_ref, npre_ref, w_ref, wut_ref, conv_ref, alog_ref, dtb_ref,
                   q_ref, k_ref, v_ref, z_ref, ut_ref, bg_ref, ext_ref, *, heads, gw):
    b = pl.program_id(0)
    i = pl.program_id(1)
    tm = x_ref.shape[1]
    x = x_ref[0]
    sh = sh_ref[pl.ds(b, 1), :]
    sc = sc_ref[pl.ds(b, 1), :]
    h = (_rms(x, npre_ref[...]) * (1.0 + sc) + sh).astype(BF16)
    proj = lambda lo, hi: jnp.dot(h, w_ref[:, lo:hi], preferred_element_type=F32)

    @pl.when(i == 0)
    def _():
        ext_ref[0:8, :] = jnp.zeros((8, 3 * gw), F32)

    for blk, ref in enumerate((q_ref, k_ref, v_ref)):
        cols = slice(blk * gw, (blk + 1) * gw)
        ext_ref[8:, cols] = proj(blk * gw, (blk + 1) * gw)
        conv = conv_ref[0:1, cols] * ext_ref[pl.ds(8 - (CONV_K - 1), tm), cols]
        for j in range(1, CONV_K):
            conv = conv + conv_ref[j:j + 1, cols] * ext_ref[pl.ds(8 - (CONV_K - 1) + j, tm), cols]
        ext_ref[0:8, cols] = ext_ref[tm:tm + 8, cols]
        act = _silu(conv)
        if ref is v_ref:
            ref[0] = act
        else:
            for hd in range(heads):
                t = act[:, hd * HEAD_DIM:(hd + 1) * HEAD_DIM]
                ref[0, :, hd * HEAD_DIM:(hd + 1) * HEAD_DIM] = t * lax.rsqrt(
                    jnp.sum(t * t, axis=-1, keepdims=True) + EPS)
    z_ref[0] = proj(3 * gw, 4 * gw)

    ut = lax.dot_general(wut_ref[...], h, (((1,), (1,)), ((), ())), preferred_element_type=F32)
    for grp in range(ut_ref.shape[0]):
        for cc in range(tm // S5_T):
            ut_ref[grp, 0, cc] = ut[grp * S5_GROUP:(grp + 1) * S5_GROUP, cc * S5_T:(cc + 1) * S5_T]

    ba = proj(4 * gw, 4 * gw + BG_LANES)
    beta = jax.nn.sigmoid(ba)
    t = ba + dtb_ref[...]
    softplus = jnp.maximum(t, 0.0) + jnp.log1p(jnp.exp(-jnp.abs(t)))
    g = -jnp.exp(alog_ref[...]) * softplus
    lane = lax.broadcasted_iota(jnp.int32, ba.shape, 1)
    bg_ref[0] = jnp.where(lane < heads, beta, g)


def _mix_in(x, mod, npre, w_in, conv_w, a_log, dt_bias, tm):
    bsz, seq, d = x.shape
    gw = d // 2
    sw = d - gw
    heads = gw // HEAD_DIM
    groups = sw // S5_GROUP
    assert 2 * heads <= BG_LANES and seq % tm == 0 and tm % S5_T == 0
    nc = seq // S5_T
    qkv_w, z_w, beta_w, a_w, u_w = jnp.split(
        w_in, [3 * gw, 4 * gw, 4 * gw + heads, 4 * gw + 2 * heads], axis=1)
    pad = jnp.zeros((d, BG_LANES - 2 * heads), w_in.dtype)
    w = jnp.concatenate([qkv_w, z_w, beta_w, a_w, pad], axis=1).astype(BF16)
    wut = u_w.T.astype(BF16)
    n = w.shape[1]
    lane_pad = (0, BG_LANES - 2 * heads)
    alog = jnp.pad(jnp.concatenate([jnp.zeros_like(a_log), a_log]), lane_pad).reshape(1, BG_LANES)
    dtb = jnp.pad(jnp.concatenate([jnp.zeros_like(dt_bias), dt_bias]), lane_pad).reshape(1, BG_LANES)
    tok = lambda width: pl.BlockSpec((1, tm, width), lambda b, i: (b, i, 0))
    shp = lambda width: jax.ShapeDtypeStruct((bsz, seq, width), F32)
    return pl.pallas_call(
        functools.partial(_mix_in_kernel, heads=heads, gw=gw),
        grid=(bsz, seq // tm),
        in_specs=[tok(d)] + _mod_specs(1, d)[:2] + [
            pl.BlockSpec((1, d), lambda b, i: (0, 0)),
            _resident((d, n), lambda b, i: (0, 0)),
            _resident((sw, d), lambda b, i: (0, 0)),
            pl.BlockSpec((CONV_K, 3 * gw), lambda b, i: (0, 0)),
            pl.BlockSpec((1, BG_LANES), lambda b, i: (0, 0)),
            pl.BlockSpec((1, BG_LANES), lambda b, i: (0, 0)),
        ],
        out_specs=[tok(gw), tok(gw), tok(gw), tok(gw),
                   pl.BlockSpec((groups, 1, tm // S5_T, S5_GROUP, S5_T), lambda b, i: (0, b, i, 0, 0)),
                   tok(BG_LANES)],
        out_shape=[shp(gw), shp(gw), shp(gw), shp(gw),
                   jax.ShapeDtypeStruct((groups, bsz, nc, S5_GROUP, S5_T), F32),
                   shp(BG_LANES)],
        scratch_shapes=[pltpu.VMEM((tm + 8, 3 * gw), F32)],
        compiler_params=_params(("arbitrary", "arbitrary")),
        name="mix_in",
    )(x, mod, mod, npre.reshape(1, d), w, wut, conv_w, alog, dtb)


def _gdn_kernel(q_ref, k_ref, v_ref, bg_ref, z_ref, nw_ref, y_ref,
                s_ref, sol_ref, attn_ref, ks_ref, qs_ref, tot_ref, *, heads, n_chunks):
    c = GDN_CHUNK
    step = pl.program_id(1)
    slot_w = lax.rem(step, 2)
    slot_r = 1 - slot_w

    @pl.when(step == 0)
    def _():
        s_ref[...] = jnp.zeros_like(s_ref)
        for ref in (sol_ref, attn_ref, ks_ref, qs_ref, tot_ref):
            ref[1] = jnp.zeros(ref.shape[1:], F32)

    row = lax.broadcasted_iota(jnp.int32, (c, c), 0)
    col = lax.broadcasted_iota(jnp.int32, (c, c), 1)
    lower_incl = row >= col
    strict = row > col
    eye = (row == col).astype(F32)
    sum_rows = lax.broadcasted_iota(jnp.int32, (c + HEAD_DIM, c), 0)
    sum_cols = lax.broadcasted_iota(jnp.int32, (c + HEAD_DIM, c), 1)
    cum_and_total = ((sum_rows >= sum_cols) | (sum_rows >= c)).astype(BF16)
    ones_c = jnp.ones((c, c), BF16)
    wide_row = lax.broadcasted_iota(jnp.int32, (c, HEAD_DIM), 0)
    wide_col = lax.broadcasted_iota(jnp.int32, (c, HEAD_DIM), 1)
    utri_wide = ((wide_row <= wide_col) & (wide_col < c)).astype(F32)
    scale = HEAD_DIM ** -0.5
    nw = nw_ref[...]

    pairs = [(ci, hd) for ci in range(n_chunks) for hd in range(heads)]
    rows_of = lambda ci: slice(ci * c, (ci + 1) * c)
    lanes_of = lambda hd: slice(hd * HEAD_DIM, (hd + 1) * HEAD_DIM)
    lane1 = lambda x, j: x[:, j:j + 1]

    pid = {p: n for n, p in enumerate(pairs)}
    t = {}

    def intra_setup():
        bgs = [bg_ref[0, rows_of(ci), :] for ci in range(n_chunks)]
        sums = [_dot_01(cum_and_total, bg) for bg in bgs]
        gc_rows = [_dot_01(ones_c, jnp.concatenate(
            [lane1(bg, heads + hd) * utri_wide for hd in range(heads)], axis=-1)) for bg in bgs]
        for ci in range(n_chunks):
            tot_ref[slot_w, ci] = sums[ci][c:]
        for p in pairs:
            ci, hd = p
            t["beta", p] = lane1(bgs[ci], hd)
            t["gc", p] = lane1(sums[ci][:c], heads + hd)
            gl = lane1(sums[ci][c:], heads + hd)
            gc_row = gc_rows[ci][:, hd * HEAD_DIM:hd * HEAD_DIM + c]
            t["decay", p] = jnp.exp(jnp.where(lower_incl, t["gc", p] - gc_row, -jnp.inf))
            t["q", p] = q_ref[0, rows_of(ci), lanes_of(hd)] * scale
            t["k", p] = k_ref[0, rows_of(ci), lanes_of(hd)]
            t["kb", p] = t["k", p] * t["beta", p]
            t["eg", p] = jnp.exp(t["gc", p])
            ks_ref[slot_w, pid[p]] = t["k", p] * jnp.exp(gl[:c] - t["gc", p])
            qs_ref[slot_w, pid[p]] = t["q", p] * t["eg", p]

    def intra_a_mat():
        for p in pairs:
            t["a", p] = jnp.where(strict, -(_dot_nt(t["kb", p], t["k", p]) * t["decay", p]), 0.0)
            t["pow", p] = t["a", p]
            t["inv", p] = eye + t["a", p]

    def intra_square():
        for p in pairs:
            t["pow", p] = _dot(t["pow", p], t["pow", p])

    def intra_extend():
        for p in pairs:
            t["inv", p] = t["inv", p] + _dot(t["inv", p], t["pow", p])

    def intra_resid():
        for p in pairs:
            t["resid", p] = (eye - t["inv", p]) + _dot_split(t["a", p], t["inv", p])

    def intra_newton():
        for p in pairs:
            t["inv", p] = t["inv", p] + _dot(t["inv", p], t["resid", p])

    def intra_solve():
        for p in pairs:
            ci, hd = p
            rhs = jnp.concatenate([v_ref[0, rows_of(ci), lanes_of(hd)] * t["beta", p],
                                   t["kb", p] * t["eg", p]], axis=-1)
            sol_ref[slot_w, pid[p]] = _dot(t["inv", p], rhs)

    def intra_attn():
        for p in pairs:
            attn_ref[slot_w, pid[p]] = _dot_nt(t["q", p], t["k", p]) * t["decay", p]

    intra = [intra_setup, intra_a_mat]
    for _ in range(int(math.log2(c)) - 2):
        intra += [intra_square, intra_extend]
    intra += [intra_resid, intra_newton, intra_solve, intra_attn]

    r = {"state": [s_ref[hd] for hd in range(heads)]}

    def inter_v_new(ci):
        def run():
            r["sb"] = [r["state"][hd].astype(BF16) for hd in range(heads)]
            r["v_new"] = []
            for hd in range(heads):
                sol = sol_ref[slot_r, pid[ci, hd]]
                r["v_new"].append(sol[:, :HEAD_DIM] - _dot(sol[:, HEAD_DIM:], r["sb"][hd]))
        return run

    def inter_out(ci):
        def run():
            new_state = []
            for hd in range(heads):
                n = pid[ci, hd]
                o = _dot(qs_ref[slot_r, n], r["sb"][hd]) + _dot(attn_ref[slot_r, n], r["v_new"][hd])
                gl = lane1(tot_ref[slot_r, ci], heads + hd)
                new_state.append(r["state"][hd] * jnp.exp(gl) + _dot_tn(ks_ref[slot_r, n], r["v_new"][hd]))
                z = z_ref[0, rows_of(ci), lanes_of(hd)]
                y_ref[0, rows_of(ci), lanes_of(hd)] = _rms(o, nw) * _silu(z)
            r["state"] = new_state
        return run

    inter = []
    for ci in range(n_chunks):
        inter += [inter_v_new(ci), inter_out(ci)]

    for n in range(max(len(intra), len(inter))):
        if n < len(intra):
            intra[n]()
        if n < len(inter):
            inter[n]()
    for hd in range(heads):
        s_ref[hd] = r["state"][hd]


def _gdn(q, k, v, z, bg, norm_w, tl):
    bsz, seq, gw = q.shape
    heads = gw // HEAD_DIM
    assert seq % tl == 0 and tl % GDN_CHUNK == 0
    n_tiles = seq // tl
    n_chunks = tl // GDN_CHUNK
    n_pairs = n_chunks * heads
    cur = lambda width: pl.BlockSpec((1, tl, width), lambda b, i: (b, jnp.minimum(i, n_tiles - 1), 0))
    prev = lambda width: pl.BlockSpec((1, tl, width), lambda b, i: (b, jnp.maximum(i - 1, 0), 0))
    slots = lambda *shape: pltpu.VMEM((2,) + shape, F32)
    return pl.pallas_call(
        functools.partial(_gdn_kernel, heads=heads, n_chunks=n_chunks),
        grid=(bsz, n_tiles + 1),
        in_specs=[cur(gw), cur(gw), cur(gw), cur(BG_LANES), prev(gw),
                  pl.BlockSpec((1, HEAD_DIM), lambda b, i: (0, 0))],
        out_specs=prev(gw),
        out_shape=jax.ShapeDtypeStruct((bsz, seq, gw), F32),
        scratch_shapes=[pltpu.VMEM((heads, HEAD_DIM, HEAD_DIM), F32),
                        slots(n_pairs, GDN_CHUNK, 2 * HEAD_DIM),
                        slots(n_pairs, GDN_CHUNK, GDN_CHUNK),
                        slots(n_pairs, GDN_CHUNK, HEAD_DIM),
                        slots(n_pairs, GDN_CHUNK, HEAD_DIM),
                        slots(n_chunks, HEAD_DIM, BG_LANES)],
        compiler_params=_params(("arbitrary", "arbitrary")),
        name="gdn",
    )(q, k, v, bg, z, norm_w.reshape(1, HEAD_DIM))


def _s5_param_kernel(are_ref, aim_ref, arc_ref, aic_ref, ldt_ref, brt_ref, bit_ref, cr_ref, ci_ref,
                     crt_ref, cit_ref, kft_ref, p_ref, g1t_ref, mul_ref, *, n_levels):
    t_len = S5_T
    ns = S5_STATE
    hw = S5_GROUP
    dt = jnp.exp(ldt_ref[0])

    ar = jnp.minimum(are_ref[0], -1e-4)
    ai = aim_ref[0]
    mag = jnp.exp(dt * ar)
    abar_re = mag * jnp.cos(dt * ai)
    abar_im = mag * jnp.sin(dt * ai)
    denom = ar * ar + ai * ai
    zr = abar_re - 1.0
    zi = abar_im
    fr = (zr * ar + zi * ai) / denom
    fi = (zi * ar - zr * ai) / denom
    brt = brt_ref[0]
    bit = bit_ref[0]
    bbar_re_t = fr * brt - fi * bit
    bbar_im_t = fr * bit + fi * brt

    tau_rev = (t_len - 1 - lax.broadcasted_iota(jnp.int32, (t_len, ns), 0)).astype(F32)
    rev_mag = jnp.exp(tau_rev * (dt * ar))
    rev_re = rev_mag * jnp.cos(tau_rev * (dt * ai))
    rev_im = rev_mag * jnp.sin(tau_rev * (dt * ai))
    for hp in range(hw):
        b_r = bbar_re_t[hp:hp + 1, :]
        b_i = bbar_im_t[hp:hp + 1, :]
        p_ref[0, hp * t_len:(hp + 1) * t_len, :] = jnp.concatenate(
            [rev_re * b_r - rev_im * b_i, rev_re * b_i + rev_im * b_r], axis=-1).astype(BF16)

    ar_c = jnp.minimum(arc_ref[0], -1e-4)
    ai_c = aic_ref[0]
    tau = lax.broadcasted_iota(jnp.int32, (ns, t_len), 1).astype(F32)

    def lam_pow(t):
        m = jnp.exp(t * (dt * ar_c))
        return m * jnp.cos(t * (dt * ai_c)), m * jnp.sin(t * (dt * ai_c))

    pow_re, pow_im = lam_pow(tau)
    rows = hw * hw
    r_idx = lax.broadcasted_iota(jnp.int32, (rows, hw), 0)
    c_idx = lax.broadcasted_iota(jnp.int32, (rows, hw), 1)
    pick_h = (_mod_pow2(r_idx, hw) == c_idx).astype(F32)
    pick_hp = (_div_pow2(r_idx, hw) == c_idx).astype(F32)
    c_r = _dot_f32(pick_h, cr_ref[0])
    c_i = _dot_f32(pick_h, ci_ref[0])
    b_r = _dot_f32(pick_hp, bbar_re_t)
    b_i = _dot_f32(pick_hp, bbar_im_t)
    cb = jnp.concatenate([c_r * b_r - c_i * b_i, -(c_r * b_i + c_i * b_r)], axis=-1)
    kft_ref[0] = _dot_f32(cb, jnp.concatenate([pow_re, pow_im], axis=0))

    nxt_re, nxt_im = lam_pow(tau + 1.0)
    crt = crt_ref[0]
    cit = cit_ref[0]
    for h in range(hw):
        c_r = crt[:, h:h + 1]
        c_i = cit[:, h:h + 1]
        g1t_ref[0, :, h * t_len:(h + 1) * t_len] = jnp.concatenate(
            [c_r * nxt_re - c_i * nxt_im, -(c_r * nxt_im + c_i * nxt_re)], axis=0)

    step_mag = jnp.exp(float(t_len) * (dt * ar))
    step_re = step_mag * jnp.cos(float(t_len) * (dt * ai))
    step_im = step_mag * jnp.sin(float(t_len) * (dt * ai))
    for lvl in range(n_levels):
        mul_ref[0, 2 * lvl:2 * lvl + 1, :] = jnp.concatenate([step_re, step_re], axis=-1)
        mul_ref[0, 2 * lvl + 1:2 * lvl + 2, :] = jnp.concatenate([-step_im, step_im], axis=-1)
        step_re, step_im = step_re * step_re - step_im * step_im, 2.0 * step_re * step_im


def _s5_main_kernel(ut_ref, kft_ref, p_ref, g1t_ref, mul_ref, d_ref, yt_ref, m_ref, a_ref,
                    *, n_chunks, n_levels):
    t_len = S5_T
    hw = S5_GROUP
    rows = a_ref.shape[0]

    causal = (lax.broadcasted_iota(jnp.int32, (t_len, t_len), 1)
              >= lax.broadcasted_iota(jnp.int32, (t_len, t_len), 0))
    def toeplitz_rows(hp, carry):
        r0 = pl.multiple_of(hp * t_len, t_len)
        for h in range(hw):
            k_row = kft_ref[0, pl.ds(hp * hw + h, 1), :]
            blk = pltpu.roll(jnp.broadcast_to(k_row, (t_len, t_len)), 0, 1, stride=1, stride_axis=0)
            m_ref[pl.ds(r0, t_len), h * t_len:(h + 1) * t_len] = jnp.where(causal, blk, 0.0).astype(BF16)
        return carry

    lax.fori_loop(0, hw, toeplitz_rows, 0)

    for hp in range(hw):
        a_ref[:, hp * t_len:(hp + 1) * t_len] = ut_ref[0, pl.ds(hp, rows, stride=hw), :]
    a = a_ref[...]
    a_bf = a.astype(BF16)
    state = jnp.dot(a_bf, p_ref[0], preferred_element_type=F32)
    chunk = _mod_pow2(lax.broadcasted_iota(jnp.int32, state.shape, 0), n_chunks)
    for lvl in range(n_levels):
        dist = 2 ** lvl
        prev = jnp.where(chunk >= dist, pltpu.roll(state, dist, axis=0), 0.0)
        state = (state + prev * mul_ref[0, 2 * lvl:2 * lvl + 1, :]
                 + pltpu.roll(prev, S5_STATE, axis=1) * mul_ref[0, 2 * lvl + 1:2 * lvl + 2, :])
    incoming = jnp.where(chunk >= 1, pltpu.roll(state, 1, axis=0), 0.0)
    y = (jnp.dot(a_bf, m_ref[...], preferred_element_type=F32)
         + _dot(incoming, g1t_ref[0]) + d_ref[0] * a)
    y = jax.nn.gelu(y)
    for h in range(hw):
        yt_ref[0, pl.ds(h, rows, stride=hw), :] = y[:, h * t_len:(h + 1) * t_len]


def _s5(ut, a_re, a_im, log_dt, b_re, b_im, c_re, c_im, d_skip):
    groups, bsz, nc, hw, t_len = ut.shape
    n_levels = max(1, int(math.ceil(math.log2(nc))))
    tw = t_len * hw
    ns = S5_STATE
    grp = lambda *tail: pl.BlockSpec((1,) + tail, lambda g: (g,) + (0,) * len(tail))
    col = lambda a: a.reshape(groups, ns, 1)
    row = lambda a: a.reshape(groups, 1, ns)

    kft, p_op, g1t, mul = pl.pallas_call(
        functools.partial(_s5_param_kernel, n_levels=n_levels),
        grid=(groups,),
        in_specs=[grp(1, ns), grp(1, ns), grp(ns, 1), grp(ns, 1), grp(1, 1),
                  grp(hw, ns), grp(hw, ns), grp(hw, ns), grp(hw, ns), grp(ns, hw), grp(ns, hw)],
        out_specs=[grp(hw * hw, t_len), grp(tw, 2 * ns), grp(2 * ns, tw), grp(2 * n_levels, 2 * ns)],
        out_shape=[jax.ShapeDtypeStruct((groups, hw * hw, t_len), F32),
                   jax.ShapeDtypeStruct((groups, tw, 2 * ns), BF16),
                   jax.ShapeDtypeStruct((groups, 2 * ns, tw), F32),
                   jax.ShapeDtypeStruct((groups, 2 * n_levels, 2 * ns), F32)],
        compiler_params=_params(("arbitrary",)),
        name="s5_params",
    )(row(a_re), row(a_im), col(a_re), col(a_im), log_dt.reshape(groups, 1, 1),
      b_re.transpose(0, 2, 1), b_im.transpose(0, 2, 1), c_re, c_im,
      c_re.transpose(0, 2, 1), c_im.transpose(0, 2, 1))

    rows = bsz * nc
    d_row = jnp.repeat(d_skip, t_len, axis=1).reshape(groups, 1, tw)
    yt = pl.pallas_call(
        functools.partial(_s5_main_kernel, n_chunks=nc, n_levels=n_levels),
        grid=(groups,),
        in_specs=[grp(rows * hw, t_len), grp(hw * hw, t_len), grp(tw, 2 * ns), grp(2 * ns, tw),
                  grp(2 * n_levels, 2 * ns), grp(1, tw)],
        out_specs=grp(rows * hw, t_len),
        out_shape=jax.ShapeDtypeStruct((groups, rows * hw, t_len), F32),
        scratch_shapes=[pltpu.VMEM((tw, tw), BF16), pltpu.VMEM((rows, tw), F32)],
        compiler_params=_params(("arbitrary",)),
        name="s5_main",
    )(ut.reshape(groups, rows * hw, t_len), kft, p_op, g1t, mul, d_row)
    return yt.reshape(groups, bsz, nc, hw, t_len)


def _mix_out_kernel(x_ref, gt_ref, yg_ref, yst_ref, wglut_ref, wout_ref, npost_ref, o_ref, yt_ref):
    b = pl.program_id(0)
    gt = gt_ref[pl.ds(b, 1), :]
    for grp in range(yst_ref.shape[0]):
        for cc in range(yst_ref.shape[2]):
            yt_ref[grp * S5_GROUP:(grp + 1) * S5_GROUP, cc * S5_T:(cc + 1) * S5_T] = yst_ref[grp, 0, cc]
    yt = yt_ref[...]
    st = yt * jax.nn.sigmoid(_dot(wglut_ref[...], yt))
    gw = yg_ref.shape[2]
    y = _dot(yg_ref[0], wout_ref[:gw, :]) + _dot_tn(st, wout_ref[gw:, :])
    o_ref[0] = x_ref[0] + gt * _rms(y, npost_ref[...])


def _mix_out(x, mod, y_gdn, yt_s5, w_glu, w_out, npost, tm):
    bsz, seq, d = x.shape
    gw = y_gdn.shape[2]
    groups, _, _, hw, t_len = yt_s5.shape
    sw = groups * hw
    tok = lambda width: pl.BlockSpec((1, tm, width), lambda b, i: (b, i, 0))
    return pl.pallas_call(
        _mix_out_kernel,
        grid=(bsz, seq // tm),
        in_specs=[tok(d), _mod_specs(1, d)[2], tok(gw),
                  pl.BlockSpec((groups, 1, tm // t_len, hw, t_len), lambda b, i: (0, b, i, 0, 0)),
                  _resident((sw, sw), lambda b, i: (0, 0)),
                  _resident((gw + sw, d), lambda b, i: (0, 0)),
                  pl.BlockSpec((1, d), lambda b, i: (0, 0))],
        out_specs=tok(d),
        out_shape=jax.ShapeDtypeStruct(x.shape, F32),
        scratch_shapes=[pltpu.VMEM((sw, tm), F32)],
        compiler_params=_params(("arbitrary", "arbitrary")),
        name="mix_out",
    )(x, mod, y_gdn, yt_s5, w_glu.T.astype(BF16), w_out.astype(BF16), npost.reshape(1, d))


def kernel(x, c, w_mod, b_mod, ff1_norm_pre, ff1_norm_post, ff1_w_in, ff1_w_out, mix_norm_pre, mix_norm_post, mix_w_in, conv_w, a_log, dt_bias, gdn_norm_w, s5_a_re, s5_a_im, s5_log_dt, s5_b_re, s5_b_im, s5_c_re, s5_c_im, s5_d, s5_w_glu, mix_w_out, ff2_norm_pre, ff2_norm_post, ff2_w_in, ff2_w_out):
    depth = w_mod.shape[0]
    seq = x.shape[1]
    tm = min(512, seq)
    tm_ffn = min(1024, seq)
    tl = min(512, seq)
    mods = _modulation(c, w_mod, b_mod)
    for l in range(depth):
        mod = mods[l]
        x = _ffn(x, mod, 0, ff1_norm_pre[l], ff1_norm_post[l], ff1_w_in[l], ff1_w_out[l], tm_ffn)
        q, k, v, z, ut, bg = _mix_in(x, mod, mix_norm_pre[l], mix_w_in[l], conv_w[l], a_log[l], dt_bias[l], tm)
        y_gdn = _gdn(q, k, v, z, bg, gdn_norm_w[l], tl)
        yt_s5 = _s5(ut, s5_a_re[l], s5_a_im[l], s5_log_dt[l], s5_b_re[l], s5_b_im[l],
                    s5_c_re[l], s5_c_im[l], s5_d[l])
        x = _mix_out(x, mod, y_gdn, yt_s5, s5_w_glu[l], mix_w_out[l], mix_norm_post[l], tm)
        x = _ffn(x, mod, 2, ff2_norm_pre[l], ff2_norm_post[l], ff2_w_in[l], ff2_w_out[l], tm_ffn)
    return x
```

```python
import functools
import math

import jax
import jax.numpy as jnp
from jax import lax
from jax.experimental import pallas as pl
from jax.experimental.pallas import tpu as pltpu

F32 = jnp.float32
BF16 = jnp.bfloat16
EPS = 1e-6

HEAD_DIM = 128
GDN_CHUNK = 64
CONV_K = 4
S5_GROUP = 16
S5_STATE = 64
N_MOD = 9
S5_T = 128
BG_LANES = 128

V7X_VMEM_LIMIT_BYTES = 56 * 1024 * 1024
HIGHEST = lax.Precision.HIGHEST


def _params(semantics):
    return pltpu.CompilerParams(dimension_semantics=semantics, vmem_limit_bytes=V7X_VMEM_LIMIT_BYTES)


def _resident(block_shape, index_map):
    return pl.BlockSpec(block_shape, index_map, pipeline_mode=pl.Buffered(1))


def _dot(a, b):
    return jnp.dot(a.astype(BF16), b.astype(BF16), preferred_element_type=F32)


def _dot_nt(a, b):
    return lax.dot_general(a.astype(BF16), b.astype(BF16), (((1,), (1,)), ((), ())),
                           preferred_element_type=F32)


def _dot_tn(a, b):
    return lax.dot_general(a.astype(BF16), b.astype(BF16), (((0,), (0,)), ((), ())),
                           preferred_element_type=F32)


def _dot_split(a, b):
    a_hi = a.astype(BF16)
    a_lo = (a - a_hi.astype(F32)).astype(BF16)
    b_hi = b.astype(BF16)
    b_lo = (b - b_hi.astype(F32)).astype(BF16)
    mm = functools.partial(jnp.dot, preferred_element_type=F32)
    return mm(a_hi, b_hi) + mm(a_hi, b_lo) + mm(a_lo, b_hi)


def _dot_01(sel, x):
    hi = x.astype(BF16)
    rest = x - hi.astype(F32)
    mid = rest.astype(BF16)
    lo = (rest - mid.astype(F32)).astype(BF16)
    mm = functools.partial(jnp.dot, preferred_element_type=F32)
    return mm(sel, hi) + mm(sel, mid) + mm(sel, lo)


def _dot_f32(a, b):
    return jnp.dot(a, b, precision=HIGHEST, preferred_element_type=F32)


def _rms(x, w):
    return x * lax.rsqrt(jnp.mean(x * x, axis=-1, keepdims=True) + EPS) * w


def _silu(x):
    return x * jax.nn.sigmoid(x)


def _div_pow2(x, n):
    assert n & (n - 1) == 0
    return jnp.right_shift(x, n.bit_length() - 1)


def _mod_pow2(x, n):
    assert n & (n - 1) == 0
    return jnp.bitwise_and(x, n - 1)


def _mod_kernel(c_ref, w_ref, b_ref, o_ref):
    c = c_ref[...]
    o_ref[0] = _dot(_silu(c), w_ref[0]) + b_ref[0]


def _modulation(c, w_mod, b_mod):
    depth, d, n = w_mod.shape
    bsz = c.shape[0]
    rows = 8
    assert bsz <= rows
    tn = n // 8
    c_pad = jnp.pad(c, ((0, rows - bsz), (0, 0)))
    return pl.pallas_call(
        _mod_kernel,
        grid=(depth, n // tn),
        in_specs=[
            pl.BlockSpec((rows, d), lambda l, j: (0, 0)),
            pl.BlockSpec((1, d, tn), lambda l, j: (l, 0, j)),
            pl.BlockSpec((1, 1, tn), lambda l, j: (l, 0, j)),
        ],
        out_specs=pl.BlockSpec((1, rows, tn), lambda l, j: (l, 0, j)),
        out_shape=jax.ShapeDtypeStruct((depth, rows, n), F32),
        compiler_params=_params(("arbitrary", "arbitrary")),
        name="adaln_mod",
    )(c_pad, w_mod, b_mod.reshape(depth, 1, n))


def _mod_specs(sub, d):
    return [pl.BlockSpec((8, d), functools.partial(lambda b, i, j: (0, j), j=3 * sub + k)) for k in range(3)]


def _ffn_kernel(x_ref, sh_ref, sc_ref, gt_ref, npre_ref, npost_ref, win_ref, wout_ref, o_ref, acc_ref,
                *, n_chunks, tf):
    b = pl.program_id(0)
    sh = sh_ref[pl.ds(b, 1), :]
    sc = sc_ref[pl.ds(b, 1), :]
    gt = gt_ref[pl.ds(b, 1), :]
    f = wout_ref.shape[0]
    n_sub, sub = acc_ref.shape[0], acc_ref.shape[1]
    rows = lambda s: slice(s * sub, (s + 1) * sub)

    def prologue(s):
        return (_rms(x_ref[0, rows(s), :], npre_ref[...]) * (1.0 + sc) + sh).astype(BF16)

    def chunk(s, h, ci):
        lo, hi = ci * tf, (ci + 1) * tf
        gate = _dot(h, win_ref[:, lo:hi])
        up = _dot(h, win_ref[:, f + lo:f + hi])
        part = _dot(_silu(gate) * up, wout_ref[lo:hi, :])
        if ci == 0:
            acc_ref[s] = part
        else:
            acc_ref[s] += part

    def epilogue(s):
        o_ref[0, rows(s), :] = x_ref[0, rows(s), :] + (0.5 * gt) * _rms(acc_ref[s], npost_ref[...])

    lead = 2
    h_cur = prologue(0)
    for s in range(n_sub):
        h_next = None
        for ci in range(n_chunks):
            chunk(s, h_cur, ci)
            if ci == n_chunks - 1 - lead and s + 1 < n_sub:
                h_next = prologue(s + 1)
            if ci == lead - 1 and s > 0:
                epilogue(s - 1)
        h_cur = h_next
    epilogue(n_sub - 1)


def _ffn(x, mod, sub, npre, npost, w_in, w_out, tm):
    bsz, seq, d = x.shape
    f = w_out.shape[0]
    tf = 256
    sub_rows = min(tm, 512)
    assert f % tf == 0 and seq % tm == 0 and tm % sub_rows == 0
    n = f // tf
    win, wout = w_in, w_out
    return pl.pallas_call(
        functools.partial(_ffn_kernel, n_chunks=n, tf=tf),
        grid=(bsz, seq // tm),
        in_specs=[pl.BlockSpec((1, tm, d), lambda b, i: (b, i, 0))] + _mod_specs(sub, d) + [
            pl.BlockSpec((1, d), lambda b, i: (0, 0)),
            pl.BlockSpec((1, d), lambda b, i: (0, 0)),
            _resident((d, 2 * f), lambda b, i: (0, 0)),
            _resident((f, d), lambda b, i: (0, 0)),
        ],
        out_specs=pl.BlockSpec((1, tm, d), lambda b, i: (b, i, 0)),
        out_shape=jax.ShapeDtypeStruct(x.shape, F32),
        scratch_shapes=[pltpu.VMEM((tm // sub_rows, sub_rows, d), F32)],
        compiler_params=_params(("arbitrary", "arbitrary")),
        name="ffn",
    )(x, mod, mod, mod, npre.reshape(1, d), npost.reshape(1, d), win, wout)


def _mix_in_kernel(x_ref, sh_ref, sc_ref, npre_ref, w_ref, wut_ref, conv_ref, alog_ref, dtb_ref,
                   q_ref, k_ref, v_ref, z_ref, ut_ref, bg_ref, ext_ref, *, heads, gw):
    b = pl.program_id(0)
    i = pl.program_id(1)
    tm = x_ref.shape[1]
    x = x_ref[0]
    sh = sh_ref[pl.ds(b, 1), :]
    sc = sc_ref[pl.ds(b, 1), :]
    inv_rms = lax.rsqrt(jnp.mean(x * x, axis=-1, keepdims=True) + EPS)
    h = (x * inv_rms * (npre_ref[...] * (1.0 + sc)) + sh).astype(BF16)
    proj = lambda lo, hi: jnp.dot(h, w_ref[:, lo:hi], preferred_element_type=F32)

    @pl.when(i == 0)
    def _():
        ext_ref[0:8, :] = jnp.zeros((8, 3 * gw), F32)

    for blk, ref in enumerate((q_ref, k_ref, v_ref)):
        cols = slice(blk * gw, (blk + 1) * gw)
        pj = proj(blk * gw, (blk + 1) * gw)
        tap = lambda j: conv_ref[j:j + 1, cols]
        body = tap(CONV_K - 1) * pj
        for dist in range(1, CONV_K):
            body = body + tap(CONV_K - 1 - dist) * pltpu.roll(pj, dist, axis=0)
        ext_ref[8:16, cols] = pj[0:8]
        head = tap(0) * ext_ref[pl.ds(8 - (CONV_K - 1), 8), cols]
        for j in range(1, CONV_K):
            head = head + tap(j) * ext_ref[pl.ds(8 - (CONV_K - 1) + j, 8), cols]
        ext_ref[0:8, cols] = pj[tm - 8:tm]
        act = _silu(jnp.concatenate([head, body[8:]], axis=0))
        if ref is v_ref:
            ref[0] = act
        else:
            for hd in range(heads):
                t = act[:, hd * HEAD_DIM:(hd + 1) * HEAD_DIM]
                ref[0, :, hd * HEAD_DIM:(hd + 1) * HEAD_DIM] = t * lax.rsqrt(
                    jnp.sum(t * t, axis=-1, keepdims=True) + EPS)
    z_ref[0] = proj(3 * gw, 4 * gw)

    ut = lax.dot_general(wut_ref[...], h, (((1,), (1,)), ((), ())), preferred_element_type=F32)
    for grp in range(ut_ref.shape[0]):
        for cc in range(tm // S5_T):
            ut_ref[grp, 0, cc] = ut[grp * S5_GROUP:(grp + 1) * S5_GROUP, cc * S5_T:(cc + 1) * S5_T]

    ba = proj(4 * gw, 4 * gw + BG_LANES)
    beta = jax.nn.sigmoid(ba)
    t = ba + dtb_ref[...]
    softplus = jnp.maximum(t, 0.0) + jnp.log1p(jnp.exp(-jnp.abs(t)))
    g = -jnp.exp(alog_ref[...]) * softplus
    lane = lax.broadcasted_iota(jnp.int32, ba.shape, 1)
    bg_ref[0] = jnp.where(lane < heads, beta, g)


def _mix_in(x, mod, npre, w_in, conv_w, a_log, dt_bias, tm):
    bsz, seq, d = x.shape
    gw = d // 2
    sw = d - gw
    heads = gw // HEAD_DIM
    groups = sw // S5_GROUP
    assert 2 * heads <= BG_LANES and seq % tm == 0 and tm % S5_T == 0
    nc = seq // S5_T
    qkv_w, z_w, beta_w, a_w, u_w = jnp.split(
        w_in, [3 * gw, 4 * gw, 4 * gw + heads, 4 * gw + 2 * heads], axis=1)
    pad = jnp.zeros((d, BG_LANES - 2 * heads), w_in.dtype)
    w = jnp.concatenate([qkv_w, z_w, beta_w, a_w, pad], axis=1).astype(BF16)
    wut = u_w.T.astype(BF16)
    n = w.shape[1]
    lane_pad = (0, BG_LANES - 2 * heads)
    alog = jnp.pad(jnp.concatenate([jnp.zeros_like(a_log), a_log]), lane_pad).reshape(1, BG_LANES)
    dtb = jnp.pad(jnp.concatenate([jnp.zeros_like(dt_bias), dt_bias]), lane_pad).reshape(1, BG_LANES)
    tok = lambda width: pl.BlockSpec((1, tm, width), lambda b, i: (b, i, 0))
    shp = lambda width: jax.ShapeDtypeStruct((bsz, seq, width), F32)
    return pl.pallas_call(
        functools.partial(_mix_in_kernel, heads=heads, gw=gw),
        grid=(bsz, seq // tm),
        in_specs=[tok(d)] + _mod_specs(1, d)[:2] + [
            pl.BlockSpec((1, d), lambda b, i: (0, 0)),
            _resident((d, n), lambda b, i: (0, 0)),
            _resident((sw, d), lambda b, i: (0, 0)),
            pl.BlockSpec((CONV_K, 3 * gw), lambda b, i: (0, 0)),
            pl.BlockSpec((1, BG_LANES), lambda b, i: (0, 0)),
            pl.BlockSpec((1, BG_LANES), lambda b, i: (0, 0)),
        ],
        out_specs=[tok(gw), tok(gw), tok(gw), tok(gw),
                   pl.BlockSpec((groups, 1, tm // S5_T, S5_GROUP, S5_T), lambda b, i: (0, b, i, 0, 0)),
                   tok(BG_LANES)],
        out_shape=[shp(gw), shp(gw), shp(gw), shp(gw),
                   jax.ShapeDtypeStruct((groups, bsz, nc, S5_GROUP, S5_T), F32),
                   shp(BG_LANES)],
        scratch_shapes=[pltpu.VMEM((16, 3 * gw), F32)],
        compiler_params=_params(("arbitrary", "arbitrary")),
        name="mix_in",
    )(x, mod, mod, npre.reshape(1, d), w, wut, conv_w, alog, dtb)


def _gdn_kernel(q_ref, k_ref, v_ref, bg_ref, z_ref, nw_ref, y_ref,
                s_ref, sol_ref, attn_ref, ks_ref, qs_ref, tot_ref, *, heads, n_chunks):
    c = GDN_CHUNK
    step = pl.program_id(1)
    slot_w = lax.rem(step, 2)
    slot_r = 1 - slot_w

    @pl.when(step == 0)
    def _():
        s_ref[...] = jnp.zeros_like(s_ref)
        for ref in (sol_ref, attn_ref, ks_ref, qs_ref, tot_ref):
            ref[1] = jnp.zeros(ref.shape[1:], F32)

    row = lax.broadcasted_iota(jnp.int32, (c, c), 0)
    col = lax.broadcasted_iota(jnp.int32, (c, c), 1)
    lower_incl = row >= col
    strict = row > col
    eye = (row == col).astype(F32)
    sum_rows = lax.broadcasted_iota(jnp.int32, (c + HEAD_DIM, c), 0)
    sum_cols = lax.broadcasted_iota(jnp.int32, (c + HEAD_DIM, c), 1)
    cum_and_total = ((sum_rows >= sum_cols) | (sum_rows >= c)).astype(BF16)
    ones_c = jnp.ones((c, c), BF16)
    wide_row = lax.broadcasted_iota(jnp.int32, (c, HEAD_DIM), 0)
    wide_col = lax.broadcasted_iota(jnp.int32, (c, HEAD_DIM), 1)
    utri_wide = ((wide_row <= wide_col) & (wide_col < c)).astype(F32)
    scale = HEAD_DIM ** -0.5
    nw = nw_ref[...]

    pairs = [(ci, hd) for ci in range(n_chunks) for hd in range(heads)]
    rows_of = lambda ci: slice(ci * c, (ci + 1) * c)
    lanes_of = lambda hd: slice(hd * HEAD_DIM, (hd + 1) * HEAD_DIM)
    lane1 = lambda x, j: x[:, j:j + 1]

    pid = {p: n for n, p in enumerate(pairs)}
    t = {}

    def intra_setup():
        bgs = [bg_ref[0, rows_of(ci), :] for ci in range(n_chunks)]
        sums = [_dot_01(cum_and_total, bg) for bg in bgs]
        gc_rows = [_dot_01(ones_c, jnp.concatenate(
            [lane1(bg, heads + hd) * utri_wide for hd in range(heads)], axis=-1)) for bg in bgs]
        for ci in range(n_chunks):
            tot_ref[slot_w, ci] = sums[ci][c:]
        for p in pairs:
            ci, hd = p
            t["beta", p] = lane1(bgs[ci], hd)
            t["gc", p] = lane1(sums[ci][:c], heads + hd)
            gl = lane1(sums[ci][c:], heads + hd)
            gc_row = gc_rows[ci][:, hd * HEAD_DIM:hd * HEAD_DIM + c]
            t["decay", p] = jnp.exp(jnp.where(lower_incl, t["gc", p] - gc_row, -jnp.inf))
            t["q", p] = q_ref[0, rows_of(ci), lanes_of(hd)] * scale
            t["k", p] = k_ref[0, rows_of(ci), lanes_of(hd)]
            t["kb", p] = t["k", p] * t["beta", p]
            t["eg", p] = jnp.exp(t["gc", p])
            ks_ref[slot_w, pid[p]] = t["k", p] * jnp.exp(gl[:c] - t["gc", p])
            qs_ref[slot_w, pid[p]] = t["q", p] * t["eg", p]

    def intra_a_mat():
        for p in pairs:
            t["a", p] = jnp.where(strict, -(_dot_nt(t["kb", p], t["k", p]) * t["decay", p]), 0.0)
            t["pow", p] = t["a", p]
            t["inv", p] = eye + t["a", p]

    def intra_square():
        for p in pairs:
            t["pow", p] = _dot(t["pow", p], t["pow", p])

    def intra_extend():
        for p in pairs:
            t["inv", p] = t["inv", p] + _dot(t["inv", p], t["pow", p])

    def intra_resid():
        for p in pairs:
            t["resid", p] = (eye - t["inv", p]) + _dot_split(t["a", p], t["inv", p])

    def intra_newton():
        for p in pairs:
            t["inv", p] = t["inv", p] + _dot(t["inv", p], t["resid", p])

    def intra_solve():
        for p in pairs:
            ci, hd = p
            rhs = jnp.concatenate([v_ref[0, rows_of(ci), lanes_of(hd)] * t["beta", p],
                                   t["kb", p] * t["eg", p]], axis=-1)
            sol_ref[slot_w, pid[p]] = _dot(t["inv", p], rhs)

    def intra_attn():
        for p in pairs:
            attn_ref[slot_w, pid[p]] = _dot_nt(t["q", p], t["k", p]) * t["decay", p]

    intra = [intra_setup, intra_a_mat]
    for _ in range(int(math.log2(c)) - 2):
        intra += [intra_square, intra_extend]
    intra += [intra_resid, intra_newton, intra_solve, intra_attn]

    r = {"state": [s_ref[hd] for hd in range(heads)]}

    def inter_v_new(ci):
        def run():
            r["sb"] = [r["state"][hd].astype(BF16) for hd in range(heads)]
            r["v_new"] = []
            for hd in range(heads):
                sol = sol_ref[slot_r, pid[ci, hd]]
                r["v_new"].append(sol[:, :HEAD_DIM] - _dot(sol[:, HEAD_DIM:], r["sb"][hd]))
        return run

    def inter_out(ci):
        def run():
            new_state = []
            for hd in range(heads):
                n = pid[ci, hd]
                o = _dot(qs_ref[slot_r, n], r["sb"][hd]) + _dot(attn_ref[slot_r, n], r["v_new"][hd])
                gl = lane1(tot_ref[slot_r, ci], heads + hd)
                new_state.append(r["state"][hd] * jnp.exp(gl) + _dot_tn(ks_ref[slot_r, n], r["v_new"][hd]))
                z = z_ref[0, rows_of(ci), lanes_of(hd)]
                y_ref[0, rows_of(ci), lanes_of(hd)] = (_rms(o, nw) * _silu(z)).astype(BF16)
            r["state"] = new_state
        return run

    inter = []
    for ci in range(n_chunks):
        inter += [inter_v_new(ci), inter_out(ci)]

    for n in range(max(len(intra), len(inter))):
        if n < len(intra):
            intra[n]()
        if n < len(inter):
            inter[n]()
    for hd in range(heads):
        s_ref[hd] = r["state"][hd]


def _gdn(q, k, v, z, bg, norm_w, tl):
    bsz, seq, gw = q.shape
    heads = gw // HEAD_DIM
    assert seq % tl == 0 and tl % GDN_CHUNK == 0
    n_tiles = seq // tl
    n_chunks = tl // GDN_CHUNK
    n_pairs = n_chunks * heads
    cur = lambda width: pl.BlockSpec((1, tl, width), lambda b, i: (b, jnp.minimum(i, n_tiles - 1), 0))
    prev = lambda width: pl.BlockSpec((1, tl, width), lambda b, i: (b, jnp.maximum(i - 1, 0), 0))
    slots = lambda *shape: pltpu.VMEM((2,) + shape, F32)
    return pl.pallas_call(
        functools.partial(_gdn_kernel, heads=heads, n_chunks=n_chunks),
        grid=(bsz, n_tiles + 1),
        in_specs=[cur(gw), cur(gw), cur(gw), cur(BG_LANES), prev(gw),
                  pl.BlockSpec((1, HEAD_DIM), lambda b, i: (0, 0))],
        out_specs=prev(gw),
        out_shape=jax.ShapeDtypeStruct((bsz, seq, gw), BF16),
        scratch_shapes=[pltpu.VMEM((heads, HEAD_DIM, HEAD_DIM), F32),
                        slots(n_pairs, GDN_CHUNK, 2 * HEAD_DIM),
                        slots(n_pairs, GDN_CHUNK, GDN_CHUNK),
                        slots(n_pairs, GDN_CHUNK, HEAD_DIM),
                        slots(n_pairs, GDN_CHUNK, HEAD_DIM),
                        slots(n_chunks, HEAD_DIM, BG_LANES)],
        compiler_params=_params(("arbitrary", "arbitrary")),
        name="gdn",
    )(q, k, v, bg, z, norm_w.reshape(1, HEAD_DIM))


def _s5_param_kernel(are_ref, aim_ref, arc_ref, aic_ref, ldt_ref, brt_ref, bit_ref, cr_ref, ci_ref,
                     crt_ref, cit_ref, kft_ref, p_ref, g1t_ref, mul_ref, *, n_levels):
    t_len = S5_T
    ns = S5_STATE
    hw = S5_GROUP
    dt = jnp.exp(ldt_ref[0])

    ar = jnp.minimum(are_ref[0], -1e-4)
    ai = aim_ref[0]
    mag = jnp.exp(dt * ar)
    abar_re = mag * jnp.cos(dt * ai)
    abar_im = mag * jnp.sin(dt * ai)
    denom = ar * ar + ai * ai
    zr = abar_re - 1.0
    zi = abar_im
    fr = (zr * ar + zi * ai) / denom
    fi = (zi * ar - zr * ai) / denom
    brt = brt_ref[0]
    bit = bit_ref[0]
    bbar_re_t = fr * brt - fi * bit
    bbar_im_t = fr * bit + fi * brt

    tau_rev = (t_len - 1 - lax.broadcasted_iota(jnp.int32, (t_len, ns), 0)).astype(F32)
    rev_mag = jnp.exp(tau_rev * (dt * ar))
    rev_re = rev_mag * jnp.cos(tau_rev * (dt * ai))
    rev_im = rev_mag * jnp.sin(tau_rev * (dt * ai))
    for hp in range(hw):
        b_r = bbar_re_t[hp:hp + 1, :]
        b_i = bbar_im_t[hp:hp + 1, :]
        p_ref[0, hp * t_len:(hp + 1) * t_len, :] = jnp.concatenate(
            [rev_re * b_r - rev_im * b_i, rev_re * b_i + rev_im * b_r], axis=-1).astype(BF16)

    ar_c = jnp.minimum(arc_ref[0], -1e-4)
    ai_c = aic_ref[0]
    tau = lax.broadcasted_iota(jnp.int32, (ns, t_len), 1).astype(F32)

    def lam_pow(t):
        m = jnp.exp(t * (dt * ar_c))
        return m * jnp.cos(t * (dt * ai_c)), m * jnp.sin(t * (dt * ai_c))

    pow_re, pow_im = lam_pow(tau)
    rows = hw * hw
    r_idx = lax.broadcasted_iota(jnp.int32, (rows, hw), 0)
    c_idx = lax.broadcasted_iota(jnp.int32, (rows, hw), 1)
    pick_h = (_mod_pow2(r_idx, hw) == c_idx).astype(F32)
    pick_hp = (_div_pow2(r_idx, hw) == c_idx).astype(F32)
    c_r = _dot_f32(pick_h, cr_ref[0])
    c_i = _dot_f32(pick_h, ci_ref[0])
    b_r = _dot_f32(pick_hp, bbar_re_t)
    b_i = _dot_f32(pick_hp, bbar_im_t)
    cb = jnp.concatenate([c_r * b_r - c_i * b_i, -(c_r * b_i + c_i * b_r)], axis=-1)
    kft_ref[0] = _dot_f32(cb, jnp.concatenate([pow_re, pow_im], axis=0))

    nxt_re, nxt_im = lam_pow(tau + 1.0)
    crt = crt_ref[0]
    cit = cit_ref[0]
    for h in range(hw):
        c_r = crt[:, h:h + 1]
        c_i = cit[:, h:h + 1]
        g1t_ref[0, :, h * t_len:(h + 1) * t_len] = jnp.concatenate(
            [c_r * nxt_re - c_i * nxt_im, -(c_r * nxt_im + c_i * nxt_re)], axis=0)

    step_mag = jnp.exp(float(t_len) * (dt * ar))
    step_re = step_mag * jnp.cos(float(t_len) * (dt * ai))
    step_im = step_mag * jnp.sin(float(t_len) * (dt * ai))
    for lvl in range(n_levels):
        mul_ref[0, 2 * lvl:2 * lvl + 1, :] = jnp.concatenate([step_re, step_re], axis=-1)
        mul_ref[0, 2 * lvl + 1:2 * lvl + 2, :] = jnp.concatenate([-step_im, step_im], axis=-1)
        step_re, step_im = step_re * step_re - step_im * step_im, 2.0 * step_re * step_im


def _s5_main_kernel(ut_ref, kft_ref, p_ref, g1t_ref, mul_ref, d_ref, yt_ref, m_ref, a_ref,
                    *, n_chunks, n_levels):
    t_len = S5_T
    hw = S5_GROUP
    rows = a_ref.shape[0]

    causal = (lax.broadcasted_iota(jnp.int32, (t_len, t_len), 1)
              >= lax.broadcasted_iota(jnp.int32, (t_len, t_len), 0))
    def toeplitz_rows(hp, carry):
        r0 = pl.multiple_of(hp * t_len, t_len)
        for h in range(hw):
            k_row = kft_ref[0, pl.ds(hp * hw + h, 1), :]
            blk = pltpu.roll(jnp.broadcast_to(k_row, (t_len, t_len)), 0, 1, stride=1, stride_axis=0)
            m_ref[pl.ds(r0, t_len), h * t_len:(h + 1) * t_len] = jnp.where(causal, blk, 0.0).astype(BF16)
        return carry

    lax.fori_loop(0, hw, toeplitz_rows, 0)

    for hp in range(hw):
        a_ref[:, hp * t_len:(hp + 1) * t_len] = ut_ref[0, pl.ds(hp, rows, stride=hw), :]
    a = a_ref[...]
    a_bf = a.astype(BF16)
    state = jnp.dot(a_bf, p_ref[0], preferred_element_type=F32)
    chunk = _mod_pow2(lax.broadcasted_iota(jnp.int32, state.shape, 0), n_chunks)
    for lvl in range(n_levels):
        dist = 2 ** lvl
        prev = jnp.where(chunk >= dist, pltpu.roll(state, dist, axis=0), 0.0)
        state = (state + prev * mul_ref[0, 2 * lvl:2 * lvl + 1, :]
                 + pltpu.roll(prev, S5_STATE, axis=1) * mul_ref[0, 2 * lvl + 1:2 * lvl + 2, :])
    incoming = jnp.where(chunk >= 1, pltpu.roll(state, 1, axis=0), 0.0)
    y = (jnp.dot(a_bf, m_ref[...], preferred_element_type=F32)
         + _dot(incoming, g1t_ref[0]) + d_ref[0] * a)
    y = jax.nn.gelu(y)
    for h in range(hw):
        yt_ref[0, pl.ds(h, rows, stride=hw), :] = y[:, h * t_len:(h + 1) * t_len]


def _s5(ut, a_re, a_im, log_dt, b_re, b_im, c_re, c_im, d_skip):
    groups, bsz, nc, hw, t_len = ut.shape
    n_levels = max(1, int(math.ceil(math.log2(nc))))
    tw = t_len * hw
    ns = S5_STATE
    grp = lambda *tail: pl.BlockSpec((1,) + tail, lambda g: (g,) + (0,) * len(tail))
    col = lambda a: a.reshape(groups, ns, 1)
    row = lambda a: a.reshape(groups, 1, ns)

    kft, p_op, g1t, mul = pl.pallas_call(
        functools.partial(_s5_param_kernel, n_levels=n_levels),
        grid=(groups,),
        in_specs=[grp(1, ns), grp(1, ns), grp(ns, 1), grp(ns, 1), grp(1, 1),
                  grp(hw, ns), grp(hw, ns), grp(hw, ns), grp(hw, ns), grp(ns, hw), grp(ns, hw)],
        out_specs=[grp(hw * hw, t_len), grp(tw, 2 * ns), grp(2 * ns, tw), grp(2 * n_levels, 2 * ns)],
        out_shape=[jax.ShapeDtypeStruct((groups, hw * hw, t_len), F32),
                   jax.ShapeDtypeStruct((groups, tw, 2 * ns), BF16),
                   jax.ShapeDtypeStruct((groups, 2 * ns, tw), F32),
                   jax.ShapeDtypeStruct((groups, 2 * n_levels, 2 * ns), F32)],
        compiler_params=_params(("arbitrary",)),
        name="s5_params",
    )(row(a_re), row(a_im), col(a_re), col(a_im), log_dt.reshape(groups, 1, 1),
      b_re.transpose(0, 2, 1), b_im.transpose(0, 2, 1), c_re, c_im,
      c_re.transpose(0, 2, 1), c_im.transpose(0, 2, 1))

    rows = bsz * nc
    d_row = jnp.repeat(d_skip, t_len, axis=1).reshape(groups, 1, tw)
    yt = pl.pallas_call(
        functools.partial(_s5_main_kernel, n_chunks=nc, n_levels=n_levels),
        grid=(groups,),
        in_specs=[grp(rows * hw, t_len), grp(hw * hw, t_len), grp(tw, 2 * ns), grp(2 * ns, tw),
                  grp(2 * n_levels, 2 * ns), grp(1, tw)],
        out_specs=grp(rows * hw, t_len),
        out_shape=jax.ShapeDtypeStruct((groups, rows * hw, t_len), F32),
        scratch_shapes=[pltpu.VMEM((tw, tw), BF16), pltpu.VMEM((rows, tw), F32)],
        compiler_params=_params(("arbitrary",)),
        name="s5_main",
    )(ut.reshape(groups, rows * hw, t_len), kft, p_op, g1t, mul, d_row)
    return yt.reshape(groups, bsz, nc, hw, t_len)


def _mix_out_kernel(x_ref, gt_ref, yg_ref, yst_ref, wglut_ref, wout_ref, npost_ref, o_ref, yt_ref):
    b = pl.program_id(0)
    gt = gt_ref[pl.ds(b, 1), :]
    for grp in range(yst_ref.shape[0]):
        for cc in range(yst_ref.shape[2]):
            yt_ref[grp * S5_GROUP:(grp + 1) * S5_GROUP, cc * S5_T:(cc + 1) * S5_T] = yst_ref[grp, 0, cc]
    yt = yt_ref[...]
    st = yt * jax.nn.sigmoid(_dot(wglut_ref[...], yt))
    gw = yg_ref.shape[2]
    y = _dot(yg_ref[0], wout_ref[:gw, :]) + _dot_tn(st, wout_ref[gw:, :])
    o_ref[0] = x_ref[0] + gt * _rms(y, npost_ref[...])


def _mix_out(x, mod, y_gdn, yt_s5, w_glu, w_out, npost, tm):
    bsz, seq, d = x.shape
    gw = y_gdn.shape[2]
    groups, _, _, hw, t_len = yt_s5.shape
    sw = groups * hw
    tok = lambda width: pl.BlockSpec((1, tm, width), lambda b, i: (b, i, 0))
    return pl.pallas_call(
        _mix_out_kernel,
        grid=(bsz, seq // tm),
        in_specs=[tok(d), _mod_specs(1, d)[2], tok(gw),
                  pl.BlockSpec((groups, 1, tm // t_len, hw, t_len), lambda b, i: (0, b, i, 0, 0)),
                  _resident((sw, sw), lambda b, i: (0, 0)),
                  _resident((gw + sw, d), lambda b, i: (0, 0)),
                  pl.BlockSpec((1, d), lambda b, i: (0, 0))],
        out_specs=tok(d),
        out_shape=jax.ShapeDtypeStruct(x.shape, F32),
        scratch_shapes=[pltpu.VMEM((sw, tm), F32)],
        compiler_params=_params(("arbitrary", "arbitrary")),
        name="mix_out",
    )(x, mod, y_gdn, yt_s5, w_glu.T.astype(BF16), w_out.astype(BF16), npost.reshape(1, d))


def kernel(x, c, w_mod, b_mod, ff1_norm_pre, ff1_norm_post, ff1_w_in, ff1_w_out, mix_norm_pre, mix_norm_post, mix_w_in, conv_w, a_log, dt_bias, gdn_norm_w, s5_a_re, s5_a_im, s5_log_dt, s5_b_re, s5_b_im, s5_c_re, s5_c_im, s5_d, s5_w_glu, mix_w_out, ff2_norm_pre, ff2_norm_post, ff2_w_in, ff2_w_out):
    depth = w_mod.shape[0]
    seq = x.shape[1]
    tm = min(512, seq)
    tm_ffn = min(512, seq)
    tm_out = min(1024, seq)
    tl = min(512, seq)
    mods = _modulation(c, w_mod, b_mod)
    for l in range(depth):
        mod = mods[l]
        x = _ffn(x, mod, 0, ff1_norm_pre[l], ff1_norm_post[l], ff1_w_in[l], ff1_w_out[l], tm_ffn)
        q, k, v, z, ut, bg = _mix_in(x, mod, mix_norm_pre[l], mix_w_in[l], conv_w[l], a_log[l], dt_bias[l], tm)
        y_gdn = _gdn(q, k, v, z, bg, gdn_norm_w[l], tl)
        yt_s5 = _s5(ut, s5_a_re[l], s5_a_im[l], s5_log_dt[l], s5_b_re[l], s5_b_im[l],
                    s5_c_re[l], s5_c_im[l], s5_d[l])
        x = _mix_out(x, mod, y_gdn, yt_s5, s5_w_glu[l], mix_w_out[l], mix_norm_post[l], tm_out)
        x = _ffn(x, mod, 2, ff2_norm_pre[l], ff2_norm_post[l], ff2_w_in[l], ff2_w_out[l], tm_ffn)
    return x
```

```python
import functools
import math

import jax
import jax.numpy as jnp
from jax import lax
from jax.experimental import pallas as pl
from jax.experimental.pallas import tpu as pltpu

F32 = jnp.float32
BF16 = jnp.bfloat16
EPS = 1e-6

HEAD_DIM = 128
GDN_CHUNK = 64
CONV_K = 4
S5_GROUP = 16
S5_STATE = 64
N_MOD = 9
S5_T = 128
BG_LANES = 128

V7X_VMEM_LIMIT_BYTES = 56 * 1024 * 1024
HIGHEST = lax.Precision.HIGHEST


def _params(semantics):
    return pltpu.CompilerParams(dimension_semantics=semantics, vmem_limit_bytes=V7X_VMEM_LIMIT_BYTES)


def _resident(block_shape, index_map):
    return pl.BlockSpec(block_shape, index_map, pipeline_mode=pl.Buffered(1))


def _dot(a, b):
    return jnp.dot(a.astype(BF16), b.astype(BF16), preferred_element_type=F32)


def _dot_nt(a, b):
    return lax.dot_general(a.astype(BF16), b.astype(BF16), (((1,), (1,)), ((), ())),
                           preferred_element_type=F32)


def _dot_tn(a, b):
    return lax.dot_general(a.astype(BF16), b.astype(BF16), (((0,), (0,)), ((), ())),
                           preferred_element_type=F32)


def _dot_split(a, b):
    a_hi = a.astype(BF16)
    a_lo = (a - a_hi.astype(F32)).astype(BF16)
    b_hi = b.astype(BF16)
    b_lo = (b - b_hi.astype(F32)).astype(BF16)
    mm = functools.partial(jnp.dot, preferred_element_type=F32)
    return mm(a_hi, b_hi) + mm(a_hi, b_lo) + mm(a_lo, b_hi)


def _dot_01(sel, x):
    hi = x.astype(BF16)
    rest = x - hi.astype(F32)
    mid = rest.astype(BF16)
    lo = (rest - mid.astype(F32)).astype(BF16)
    mm = functools.partial(jnp.dot, preferred_element_type=F32)
    return mm(sel, hi) + mm(sel, mid) + mm(sel, lo)


def _dot_f32(a, b):
    return jnp.dot(a, b, precision=HIGHEST, preferred_element_type=F32)


def _rms(x, w):
    return x * lax.rsqrt(jnp.mean(x * x, axis=-1, keepdims=True) + EPS) * w


def _silu(x):
    return x * jax.nn.sigmoid(x)


def _div_pow2(x, n):
    assert n & (n - 1) == 0
    return jnp.right_shift(x, n.bit_length() - 1)


def _mod_pow2(x, n):
    assert n & (n - 1) == 0
    return jnp.bitwise_and(x, n - 1)


def _mod_kernel(c_ref, w_ref, b_ref, o_ref):
    c = c_ref[...]
    o_ref[0] = _dot(_silu(c), w_ref[0]) + b_ref[0]


def _modulation(c, w_mod, b_mod):
    depth, d, n = w_mod.shape
    bsz = c.shape[0]
    rows = 8
    assert bsz <= rows
    tn = n // 8
    c_pad = jnp.pad(c, ((0, rows - bsz), (0, 0)))
    return pl.pallas_call(
        _mod_kernel,
        grid=(depth, n // tn),
        in_specs=[
            pl.BlockSpec((rows, d), lambda l, j: (0, 0)),
            pl.BlockSpec((1, d, tn), lambda l, j: (l, 0, j)),
            pl.BlockSpec((1, 1, tn), lambda l, j: (l, 0, j)),
        ],
        out_specs=pl.BlockSpec((1, rows, tn), lambda l, j: (l, 0, j)),
        out_shape=jax.ShapeDtypeStruct((depth, rows, n), F32),
        compiler_params=_params(("arbitrary", "arbitrary")),
        name="adaln_mod",
    )(c_pad, w_mod, b_mod.reshape(depth, 1, n))


def _mod_specs(sub, d):
    return [pl.BlockSpec((8, d), functools.partial(lambda b, i, j: (0, j), j=3 * sub + k)) for k in range(3)]


def _ffn_kernel(x_ref, sh_ref, sc_ref, gt_ref, npre_ref, npost_ref, win_ref, wout_ref, o_ref, acc_ref,
                *, n_chunks, tf):
    b = pl.program_id(0)
    sh = sh_ref[pl.ds(b, 1), :]
    sc = sc_ref[pl.ds(b, 1), :]
    gt = gt_ref[pl.ds(b, 1), :]
    f = wout_ref.shape[0]
    n_sub, sub = acc_ref.shape[0], acc_ref.shape[1]
    rows = lambda s: slice(s * sub, (s + 1) * sub)

    def prologue(s):
        return (_rms(x_ref[0, rows(s), :], npre_ref[...]) * (1.0 + sc) + sh).astype(BF16)

    def chunk(s, h, ci):
        lo, hi = ci * tf, (ci + 1) * tf
        gate = _dot(h, win_ref[:, lo:hi])
        up = _dot(h, win_ref[:, f + lo:f + hi])
        part = _dot(_silu(gate) * up, wout_ref[lo:hi, :])
        if ci == 0:
            acc_ref[s] = part
        else:
            acc_ref[s] += part

    def epilogue(s):
        o_ref[0, rows(s), :] = x_ref[0, rows(s), :] + (0.5 * gt) * _rms(acc_ref[s], npost_ref[...])

    lead = 2
    h_cur = prologue(0)
    for s in range(n_sub):
        h_next = None
        for ci in range(n_chunks):
            chunk(s, h_cur, ci)
            if ci == n_chunks - 1 - lead and s + 1 < n_sub:
                h_next = prologue(s + 1)
            if ci == lead - 1 and s > 0:
                epilogue(s - 1)
        h_cur = h_next
    epilogue(n_sub - 1)


def _ffn(x, mod, sub, npre, npost, w_in, w_out, layer, tm):
    bsz, seq, d = x.shape
    f = w_out.shape[1]
    tf = 256
    sub_rows = min(tm, 512)
    assert f % tf == 0 and seq % tm == 0 and tm % sub_rows == 0
    n = f // tf
    win, wout = w_in, w_out
    return pl.pallas_call(
        functools.partial(_ffn_kernel, n_chunks=n, tf=tf),
        grid=(bsz, seq // tm),
        in_specs=[pl.BlockSpec((1, tm, d), lambda b, i: (b, i, 0))] + _mod_specs(sub, d) + [
            pl.BlockSpec((1, d), lambda b, i: (0, 0)),
            pl.BlockSpec((1, d), lambda b, i: (0, 0)),
            _resident((None, d, 2 * f), lambda b, i: (layer, 0, 0)),
            _resident((None, f, d), lambda b, i: (layer, 0, 0)),
        ],
        out_specs=pl.BlockSpec((1, tm, d), lambda b, i: (b, i, 0)),
        out_shape=jax.ShapeDtypeStruct(x.shape, F32),
        scratch_shapes=[pltpu.VMEM((tm // sub_rows, sub_rows, d), F32)],
        compiler_params=_params(("arbitrary", "arbitrary")),
        name="ffn",
    )(x, mod, mod, mod, npre.reshape(1, d), npost.reshape(1, d), win, wout)


def _mix_in_kernel(x_ref, sh_ref, sc_ref, npre_ref, w_ref, wut_ref, conv_ref, alog_ref, dtb_ref,
                   q_ref, k_ref, v_ref, z_ref, ut_ref, bg_ref, ext_ref, *, heads, gw):
    b = pl.program_id(0)
    i = pl.program_id(1)
    tm = x_ref.shape[1]
    x = x_ref[0]
    sh = sh_ref[pl.ds(b, 1), :]
    sc = sc_ref[pl.ds(b, 1), :]
    inv_rms = lax.rsqrt(jnp.mean(x * x, axis=-1, keepdims=True) + EPS)
    h = (x * inv_rms * (npre_ref[...] * (1.0 + sc)) + sh).astype(BF16)
    proj = lambda lo, hi: jnp.dot(h, w_ref[:, lo:hi], preferred_element_type=F32)

    @pl.when(i == 0)
    def _():
        ext_ref[0:8, :] = jnp.zeros((8, 3 * gw), F32)

    for blk, ref in enumerate((q_ref, k_ref, v_ref)):
        cols = slice(blk * gw, (blk + 1) * gw)
        pj = proj(blk * gw, (blk + 1) * gw)
        tap = lambda j: conv_ref[j:j + 1, cols]
        body = tap(CONV_K - 1) * pj
        for dist in range(1, CONV_K):
            body = body + tap(CONV_K - 1 - dist) * pltpu.roll(pj, dist, axis=0)
        ext_ref[8:16, cols] = pj[0:8]
        head = tap(0) * ext_ref[pl.ds(8 - (CONV_K - 1), 8), cols]
        for j in range(1, CONV_K):
            head = head + tap(j) * ext_ref[pl.ds(8 - (CONV_K - 1) + j, 8), cols]
        ext_ref[0:8, cols] = pj[tm - 8:tm]
        act = _silu(jnp.concatenate([head, body[8:]], axis=0))
        if ref is v_ref:
            ref[0] = act
        else:
            for hd in range(heads):
                t = act[:, hd * HEAD_DIM:(hd + 1) * HEAD_DIM]
                ref[0, :, hd * HEAD_DIM:(hd + 1) * HEAD_DIM] = t * lax.rsqrt(
                    jnp.sum(t * t, axis=-1, keepdims=True) + EPS)
    z_ref[0] = proj(3 * gw, 4 * gw)

    ut = lax.dot_general(wut_ref[...], h, (((1,), (1,)), ((), ())), preferred_element_type=F32)
    for grp in range(ut_ref.shape[0]):
        for cc in range(tm // S5_T):
            ut_ref[grp, 0, cc] = ut[grp * S5_GROUP:(grp + 1) * S5_GROUP, cc * S5_T:(cc + 1) * S5_T]

    ba = proj(4 * gw, 4 * gw + BG_LANES)
    beta = jax.nn.sigmoid(ba)
    t = ba + dtb_ref[...]
    softplus = jnp.maximum(t, 0.0) + jnp.log1p(jnp.exp(-jnp.abs(t)))
    g = -jnp.exp(alog_ref[...]) * softplus
    lane = lax.broadcasted_iota(jnp.int32, ba.shape, 1)
    bg_ref[0] = jnp.where(lane < heads, beta, g)


def _mix_in(x, mod, npre, w_in, conv_w, a_log, dt_bias, tm):
    bsz, seq, d = x.shape
    gw = d // 2
    sw = d - gw
    heads = gw // HEAD_DIM
    groups = sw // S5_GROUP
    assert 2 * heads <= BG_LANES and seq % tm == 0 and tm % S5_T == 0
    nc = seq // S5_T
    qkv_w, z_w, beta_w, a_w, u_w = jnp.split(
        w_in, [3 * gw, 4 * gw, 4 * gw + heads, 4 * gw + 2 * heads], axis=1)
    pad = jnp.zeros((d, BG_LANES - 2 * heads), w_in.dtype)
    w = jnp.concatenate([qkv_w, z_w, beta_w, a_w, pad], axis=1).astype(BF16)
    wut = u_w.T.astype(BF16)
    n = w.shape[1]
    lane_pad = (0, BG_LANES - 2 * heads)
    alog = jnp.pad(jnp.concatenate([jnp.zeros_like(a_log), a_log]), lane_pad).reshape(1, BG_LANES)
    dtb = jnp.pad(jnp.concatenate([jnp.zeros_like(dt_bias), dt_bias]), lane_pad).reshape(1, BG_LANES)
    tok = lambda width: pl.BlockSpec((1, tm, width), lambda b, i: (b, i, 0))
    shp = lambda width: jax.ShapeDtypeStruct((bsz, seq, width), F32)
    return pl.pallas_call(
        functools.partial(_mix_in_kernel, heads=heads, gw=gw),
        grid=(bsz, seq // tm),
        in_specs=[tok(d)] + _mod_specs(1, d)[:2] + [
            pl.BlockSpec((1, d), lambda b, i: (0, 0)),
            _resident((d, n), lambda b, i: (0, 0)),
            _resident((sw, d), lambda b, i: (0, 0)),
            pl.BlockSpec((CONV_K, 3 * gw), lambda b, i: (0, 0)),
            pl.BlockSpec((1, BG_LANES), lambda b, i: (0, 0)),
            pl.BlockSpec((1, BG_LANES), lambda b, i: (0, 0)),
        ],
        out_specs=[tok(gw), tok(gw), tok(gw), tok(gw),
                   pl.BlockSpec((groups, 1, tm // S5_T, S5_GROUP, S5_T), lambda b, i: (0, b, i, 0, 0)),
                   tok(BG_LANES)],
        out_shape=[shp(gw), shp(gw), shp(gw), shp(gw),
                   jax.ShapeDtypeStruct((groups, bsz, nc, S5_GROUP, S5_T), F32),
                   shp(BG_LANES)],
        scratch_shapes=[pltpu.VMEM((16, 3 * gw), F32)],
        compiler_params=_params(("arbitrary", "arbitrary")),
        name="mix_in",
    )(x, mod, mod, npre.reshape(1, d), w, wut, conv_w, alog, dtb)


def _gdn_kernel(q_ref, k_ref, v_ref, bg_ref, z_ref, nw_ref, y_ref,
                s_ref, sol_ref, attn_ref, ks_ref, qs_ref, tot_ref, *, heads, n_chunks):
    c = GDN_CHUNK
    step = pl.program_id(1)
    slot_w = lax.rem(step, 2)
    slot_r = 1 - slot_w

    @pl.when(step == 0)
    def _():
        s_ref[...] = jnp.zeros_like(s_ref)
        for ref in (sol_ref, attn_ref, ks_ref, qs_ref, tot_ref):
            ref[1] = jnp.zeros(ref.shape[1:], F32)

    row = lax.broadcasted_iota(jnp.int32, (c, c), 0)
    col = lax.broadcasted_iota(jnp.int32, (c, c), 1)
    lower_incl = row >= col
    strict = row > col
    eye = (row == col).astype(F32)
    sum_rows = lax.broadcasted_iota(jnp.int32, (c + HEAD_DIM, c), 0)
    sum_cols = lax.broadcasted_iota(jnp.int32, (c + HEAD_DIM, c), 1)
    cum_and_total = ((sum_rows >= sum_cols) | (sum_rows >= c)).astype(BF16)
    ones_c = jnp.ones((c, c), BF16)
    wide_row = lax.broadcasted_iota(jnp.int32, (c, HEAD_DIM), 0)
    wide_col = lax.broadcasted_iota(jnp.int32, (c, HEAD_DIM), 1)
    utri_wide = ((wide_row <= wide_col) & (wide_col < c)).astype(F32)
    scale = HEAD_DIM ** -0.5
    nw = nw_ref[...]

    pairs = [(ci, hd) for ci in range(n_chunks) for hd in range(heads)]
    rows_of = lambda ci: slice(ci * c, (ci + 1) * c)
    lanes_of = lambda hd: slice(hd * HEAD_DIM, (hd + 1) * HEAD_DIM)
    lane1 = lambda x, j: x[:, j:j + 1]

    pid = {p: n for n, p in enumerate(pairs)}
    t = {}

    def intra_setup():
        bgs = [bg_ref[0, rows_of(ci), :] for ci in range(n_chunks)]
        sums = [_dot_01(cum_and_total, bg) for bg in bgs]
        gc_rows = [_dot_01(ones_c, jnp.concatenate(
            [lane1(bg, heads + hd) * utri_wide for hd in range(heads)], axis=-1)) for bg in bgs]
        for ci in range(n_chunks):
            tot_ref[slot_w, ci] = sums[ci][c:]
        for p in pairs:
            ci, hd = p
            t["beta", p] = lane1(bgs[ci], hd)
            t["gc", p] = lane1(sums[ci][:c], heads + hd)
            gl = lane1(sums[ci][c:], heads + hd)
            gc_row = gc_rows[ci][:, hd * HEAD_DIM:hd * HEAD_DIM + c]
            t["decay", p] = jnp.exp(jnp.where(lower_incl, t["gc", p] - gc_row, -jnp.inf))
            t["q", p] = q_ref[0, rows_of(ci), lanes_of(hd)] * scale
            t["k", p] = k_ref[0, rows_of(ci), lanes_of(hd)]
            t["kb", p] = t["k", p] * t["beta", p]
            t["eg", p] = jnp.exp(t["gc", p])
            ks_ref[slot_w, pid[p]] = t["k", p] * jnp.exp(gl[:c] - t["gc", p])
            qs_ref[slot_w, pid[p]] = t["q", p] * t["eg", p]

    def intra_a_mat():
        for p in pairs:
            t["a", p] = jnp.where(strict, -(_dot_nt(t["kb", p], t["k", p]) * t["decay", p]), 0.0)
            t["pow", p] = t["a", p]
            t["inv", p] = eye + t["a", p]

    def intra_square():
        for p in pairs:
            t["pow", p] = _dot(t["pow", p], t["pow", p])

    def intra_extend():
        for p in pairs:
            t["inv", p] = t["inv", p] + _dot(t["inv", p], t["pow", p])

    def intra_resid():
        for p in pairs:
            t["resid", p] = (eye - t["inv", p]) + _dot_split(t["a", p], t["inv", p])

    def intra_newton():
        for p in pairs:
            t["inv", p] = t["inv", p] + _dot(t["inv", p], t["resid", p])

    def intra_solve():
        for p in pairs:
            ci, hd = p
            rhs = jnp.concatenate([v_ref[0, rows_of(ci), lanes_of(hd)] * t["beta", p],
                                   t["kb", p] * t["eg", p]], axis=-1)
            sol_ref[slot_w, pid[p]] = _dot(t["inv", p], rhs)

    def intra_attn():
        for p in pairs:
            attn_ref[slot_w, pid[p]] = _dot_nt(t["q", p], t["k", p]) * t["decay", p]

    intra = [intra_setup, intra_a_mat]
    for _ in range(int(math.log2(c)) - 2):
        intra += [intra_square, intra_extend]
    intra += [intra_resid, intra_newton, intra_solve, intra_attn]

    r = {"state": [s_ref[hd] for hd in range(heads)]}

    def inter_v_new(ci):
        def run():
            r["sb"] = [r["state"][hd].astype(BF16) for hd in range(heads)]
            r["v_new"] = []
            for hd in range(heads):
                sol = sol_ref[slot_r, pid[ci, hd]]
                r["v_new"].append(sol[:, :HEAD_DIM] - _dot(sol[:, HEAD_DIM:], r["sb"][hd]))
        return run

    def inter_out(ci):
        def run():
            new_state = []
            for hd in range(heads):
                n = pid[ci, hd]
                o = _dot(qs_ref[slot_r, n], r["sb"][hd]) + _dot(attn_ref[slot_r, n], r["v_new"][hd])
                gl = lane1(tot_ref[slot_r, ci], heads + hd)
                new_state.append(r["state"][hd] * jnp.exp(gl) + _dot_tn(ks_ref[slot_r, n], r["v_new"][hd]))
                z = z_ref[0, rows_of(ci), lanes_of(hd)]
                y_ref[0, rows_of(ci), lanes_of(hd)] = (_rms(o, nw) * _silu(z)).astype(BF16)
            r["state"] = new_state
        return run

    inter = []
    for ci in range(n_chunks):
        inter += [inter_v_new(ci), inter_out(ci)]

    for n in range(max(len(intra), len(inter))):
        if n < len(intra):
            intra[n]()
        if n < len(inter):
            inter[n]()
    for hd in range(heads):
        s_ref[hd] = r["state"][hd]


def _gdn(q, k, v, z, bg, norm_w, tl):
    bsz, seq, gw = q.shape
    heads = gw // HEAD_DIM
    assert seq % tl == 0 and tl % GDN_CHUNK == 0
    n_tiles = seq // tl
    n_chunks = tl // GDN_CHUNK
    n_pairs = n_chunks * heads
    cur = lambda width: pl.BlockSpec((1, tl, width), lambda b, i: (b, jnp.minimum(i, n_tiles - 1), 0))
    prev = lambda width: pl.BlockSpec((1, tl, width), lambda b, i: (b, jnp.maximum(i - 1, 0), 0))
    slots = lambda *shape: pltpu.VMEM((2,) + shape, F32)
    return pl.pallas_call(
        functools.partial(_gdn_kernel, heads=heads, n_chunks=n_chunks),
        grid=(bsz, n_tiles + 1),
        in_specs=[cur(gw), cur(gw), cur(gw), cur(BG_LANES), prev(gw),
                  pl.BlockSpec((1, HEAD_DIM), lambda b, i: (0, 0))],
        out_specs=prev(gw),
        out_shape=jax.ShapeDtypeStruct((bsz, seq, gw), BF16),
        scratch_shapes=[pltpu.VMEM((heads, HEAD_DIM, HEAD_DIM), F32),
                        slots(n_pairs, GDN_CHUNK, 2 * HEAD_DIM),
                        slots(n_pairs, GDN_CHUNK, GDN_CHUNK),
                        slots(n_pairs, GDN_CHUNK, HEAD_DIM),
                        slots(n_pairs, GDN_CHUNK, HEAD_DIM),
                        slots(n_chunks, HEAD_DIM, BG_LANES)],
        compiler_params=_params(("arbitrary", "arbitrary")),
        name="gdn",
    )(q, k, v, bg, z, norm_w.reshape(1, HEAD_DIM))


def _s5_param_kernel(are_ref, aim_ref, arc_ref, aic_ref, ldt_ref, brt_ref, bit_ref, cr_ref, ci_ref,
                     crt_ref, cit_ref, kft_ref, p_ref, g1t_ref, mul_ref, *, n_levels):
    t_len = S5_T
    ns = S5_STATE
    hw = S5_GROUP
    dt = jnp.exp(ldt_ref[0])

    ar = jnp.minimum(are_ref[0], -1e-4)
    ai = aim_ref[0]
    mag = jnp.exp(dt * ar)
    abar_re = mag * jnp.cos(dt * ai)
    abar_im = mag * jnp.sin(dt * ai)
    denom = ar * ar + ai * ai
    zr = abar_re - 1.0
    zi = abar_im
    fr = (zr * ar + zi * ai) / denom
    fi = (zi * ar - zr * ai) / denom
    brt = brt_ref[0]
    bit = bit_ref[0]
    bbar_re_t = fr * brt - fi * bit
    bbar_im_t = fr * bit + fi * brt

    tau_rev = (t_len - 1 - lax.broadcasted_iota(jnp.int32, (t_len, ns), 0)).astype(F32)
    rev_mag = jnp.exp(tau_rev * (dt * ar))
    rev_re = rev_mag * jnp.cos(tau_rev * (dt * ai))
    rev_im = rev_mag * jnp.sin(tau_rev * (dt * ai))
    for hp in range(hw):
        b_r = bbar_re_t[hp:hp + 1, :]
        b_i = bbar_im_t[hp:hp + 1, :]
        p_ref[0, hp * t_len:(hp + 1) * t_len, :] = jnp.concatenate(
            [rev_re * b_r - rev_im * b_i, rev_re * b_i + rev_im * b_r], axis=-1).astype(BF16)

    ar_c = jnp.minimum(arc_ref[0], -1e-4)
    ai_c = aic_ref[0]
    tau = lax.broadcasted_iota(jnp.int32, (ns, t_len), 1).astype(F32)

    def lam_pow(t):
        m = jnp.exp(t * (dt * ar_c))
        return m * jnp.cos(t * (dt * ai_c)), m * jnp.sin(t * (dt * ai_c))

    pow_re, pow_im = lam_pow(tau)
    rows = hw * hw
    r_idx = lax.broadcasted_iota(jnp.int32, (rows, hw), 0)
    c_idx = lax.broadcasted_iota(jnp.int32, (rows, hw), 1)
    pick_h = (_mod_pow2(r_idx, hw) == c_idx).astype(F32)
    pick_hp = (_div_pow2(r_idx, hw) == c_idx).astype(F32)
    c_r = _dot_f32(pick_h, cr_ref[0])
    c_i = _dot_f32(pick_h, ci_ref[0])
    b_r = _dot_f32(pick_hp, bbar_re_t)
    b_i = _dot_f32(pick_hp, bbar_im_t)
    cb = jnp.concatenate([c_r * b_r - c_i * b_i, -(c_r * b_i + c_i * b_r)], axis=-1)
    kft_ref[0] = _dot_f32(cb, jnp.concatenate([pow_re, pow_im], axis=0))

    nxt_re, nxt_im = lam_pow(tau + 1.0)
    crt = crt_ref[0]
    cit = cit_ref[0]
    for h in range(hw):
        c_r = crt[:, h:h + 1]
        c_i = cit[:, h:h + 1]
        g1t_ref[0, :, h * t_len:(h + 1) * t_len] = jnp.concatenate(
            [c_r * nxt_re - c_i * nxt_im, -(c_r * nxt_im + c_i * nxt_re)], axis=0)

    step_mag = jnp.exp(float(t_len) * (dt * ar))
    step_re = step_mag * jnp.cos(float(t_len) * (dt * ai))
    step_im = step_mag * jnp.sin(float(t_len) * (dt * ai))
    for lvl in range(n_levels):
        mul_ref[0, 2 * lvl:2 * lvl + 1, :] = jnp.concatenate([step_re, step_re], axis=-1)
        mul_ref[0, 2 * lvl + 1:2 * lvl + 2, :] = jnp.concatenate([-step_im, step_im], axis=-1)
        step_re, step_im = step_re * step_re - step_im * step_im, 2.0 * step_re * step_im


def _s5_main_kernel(ut_ref, kft_ref, p_ref, g1t_ref, mul_ref, d_ref, yt_ref, m_ref, a_ref,
                    *, n_chunks, n_levels):
    t_len = S5_T
    hw = S5_GROUP
    rows = a_ref.shape[0]

    causal = (lax.broadcasted_iota(jnp.int32, (t_len, t_len), 1)
              >= lax.broadcasted_iota(jnp.int32, (t_len, t_len), 0))
    def toeplitz_rows(hp, carry):
        r0 = pl.multiple_of(hp * t_len, t_len)
        for h in range(hw):
            k_row = kft_ref[0, pl.ds(hp * hw + h, 1), :]
            blk = pltpu.roll(jnp.broadcast_to(k_row, (t_len, t_len)), 0, 1, stride=1, stride_axis=0)
            m_ref[pl.ds(r0, t_len), h * t_len:(h + 1) * t_len] = jnp.where(causal, blk, 0.0).astype(BF16)
        return carry

    lax.fori_loop(0, hw, toeplitz_rows, 0)

    for hp in range(hw):
        a_ref[:, hp * t_len:(hp + 1) * t_len] = ut_ref[0, pl.ds(hp, rows, stride=hw), :]
    a = a_ref[...]
    a_bf = a.astype(BF16)
    state = jnp.dot(a_bf, p_ref[0], preferred_element_type=F32)
    chunk = _mod_pow2(lax.broadcasted_iota(jnp.int32, state.shape, 0), n_chunks)
    for lvl in range(n_levels):
        dist = 2 ** lvl
        prev = jnp.where(chunk >= dist, pltpu.roll(state, dist, axis=0), 0.0)
        state = (state + prev * mul_ref[0, 2 * lvl:2 * lvl + 1, :]
                 + pltpu.roll(prev, S5_STATE, axis=1) * mul_ref[0, 2 * lvl + 1:2 * lvl + 2, :])
    incoming = jnp.where(chunk >= 1, pltpu.roll(state, 1, axis=0), 0.0)
    y = (jnp.dot(a_bf, m_ref[...], preferred_element_type=F32)
         + _dot(incoming, g1t_ref[0]) + d_ref[0] * a)
    y = jax.nn.gelu(y)
    for h in range(hw):
        yt_ref[0, pl.ds(h, rows, stride=hw), :] = y[:, h * t_len:(h + 1) * t_len]


def _s5(ut, a_re, a_im, log_dt, b_re, b_im, c_re, c_im, d_skip):
    groups, bsz, nc, hw, t_len = ut.shape
    n_levels = max(1, int(math.ceil(math.log2(nc))))
    tw = t_len * hw
    ns = S5_STATE
    grp = lambda *tail: pl.BlockSpec((1,) + tail, lambda g: (g,) + (0,) * len(tail))
    col = lambda a: a.reshape(groups, ns, 1)
    row = lambda a: a.reshape(groups, 1, ns)

    kft, p_op, g1t, mul = pl.pallas_call(
        functools.partial(_s5_param_kernel, n_levels=n_levels),
        grid=(groups,),
        in_specs=[grp(1, ns), grp(1, ns), grp(ns, 1), grp(ns, 1), grp(1, 1),
                  grp(hw, ns), grp(hw, ns), grp(hw, ns), grp(hw, ns), grp(ns, hw), grp(ns, hw)],
        out_specs=[grp(hw * hw, t_len), grp(tw, 2 * ns), grp(2 * ns, tw), grp(2 * n_levels, 2 * ns)],
        out_shape=[jax.ShapeDtypeStruct((groups, hw * hw, t_len), F32),
                   jax.ShapeDtypeStruct((groups, tw, 2 * ns), BF16),
                   jax.ShapeDtypeStruct((groups, 2 * ns, tw), F32),
                   jax.ShapeDtypeStruct((groups, 2 * n_levels, 2 * ns), F32)],
        compiler_params=_params(("arbitrary",)),
        name="s5_params",
    )(row(a_re), row(a_im), col(a_re), col(a_im), log_dt.reshape(groups, 1, 1),
      b_re.transpose(0, 2, 1), b_im.transpose(0, 2, 1), c_re, c_im,
      c_re.transpose(0, 2, 1), c_im.transpose(0, 2, 1))

    rows = bsz * nc
    d_row = jnp.repeat(d_skip, t_len, axis=1).reshape(groups, 1, tw)
    yt = pl.pallas_call(
        functools.partial(_s5_main_kernel, n_chunks=nc, n_levels=n_levels),
        grid=(groups,),
        in_specs=[grp(rows * hw, t_len), grp(hw * hw, t_len), grp(tw, 2 * ns), grp(2 * ns, tw),
                  grp(2 * n_levels, 2 * ns), grp(1, tw)],
        out_specs=grp(rows * hw, t_len),
        out_shape=jax.ShapeDtypeStruct((groups, rows * hw, t_len), F32),
        scratch_shapes=[pltpu.VMEM((tw, tw), BF16), pltpu.VMEM((rows, tw), F32)],
        compiler_params=_params(("arbitrary",)),
        name="s5_main",
    )(ut.reshape(groups, rows * hw, t_len), kft, p_op, g1t, mul, d_row)
    return yt.reshape(groups, bsz, nc, hw, t_len)


def _mix_out_kernel(x_ref, gt_ref, yg_ref, yst_ref, wglut_ref, wout_ref, npost_ref, o_ref, yt_ref):
    b = pl.program_id(0)
    gt = gt_ref[pl.ds(b, 1), :]
    for grp in range(yst_ref.shape[0]):
        for cc in range(yst_ref.shape[2]):
            yt_ref[grp * S5_GROUP:(grp + 1) * S5_GROUP, cc * S5_T:(cc + 1) * S5_T] = yst_ref[grp, 0, cc]
    yt = yt_ref[...]
    st = yt * jax.nn.sigmoid(_dot(wglut_ref[...], yt))
    gw = yg_ref.shape[2]
    y = _dot(yg_ref[0], wout_ref[:gw, :]) + _dot_tn(st, wout_ref[gw:, :])
    o_ref[0] = x_ref[0] + gt * _rms(y, npost_ref[...])


def _mix_out(x, mod, y_gdn, yt_s5, w_glu, w_out, npost, tm):
    bsz, seq, d = x.shape
    gw = y_gdn.shape[2]
    groups, _, _, hw, t_len = yt_s5.shape
    sw = groups * hw
    tok = lambda width: pl.BlockSpec((1, tm, width), lambda b, i: (b, i, 0))
    return pl.pallas_call(
        _mix_out_kernel,
        grid=(bsz, seq // tm),
        in_specs=[tok(d), _mod_specs(1, d)[2], tok(gw),
                  pl.BlockSpec((groups, 1, tm // t_len, hw, t_len), lambda b, i: (0, b, i, 0, 0)),
                  _resident((sw, sw), lambda b, i: (0, 0)),
                  _resident((gw + sw, d), lambda b, i: (0, 0)),
                  pl.BlockSpec((1, d), lambda b, i: (0, 0))],
        out_specs=tok(d),
        out_shape=jax.ShapeDtypeStruct(x.shape, F32),
        scratch_shapes=[pltpu.VMEM((sw, tm), F32)],
        compiler_params=_params(("arbitrary", "arbitrary")),
        name="mix_out",
    )(x, mod, y_gdn, yt_s5, w_glu.T.astype(BF16), w_out.astype(BF16), npost.reshape(1, d))


def kernel(x, c, w_mod, b_mod, ff1_norm_pre, ff1_norm_post, ff1_w_in, ff1_w_out, mix_norm_pre, mix_norm_post, mix_w_in, conv_w, a_log, dt_bias, gdn_norm_w, s5_a_re, s5_a_im, s5_log_dt, s5_b_re, s5_b_im, s5_c_re, s5_c_im, s5_d, s5_w_glu, mix_w_out, ff2_norm_pre, ff2_norm_post, ff2_w_in, ff2_w_out):
    depth = w_mod.shape[0]
    seq = x.shape[1]
    tm = min(512, seq)
    tm_ffn = min(512, seq)
    tm_out = min(1024, seq)
    tl = min(512, seq)
    mods = _modulation(c, w_mod, b_mod)
    for l in range(depth):
        mod = mods[l]
        x = _ffn(x, mod, 0, ff1_norm_pre[l], ff1_norm_post[l], ff1_w_in, ff1_w_out, l, tm_ffn)
        q, k, v, z, ut, bg = _mix_in(x, mod, mix_norm_pre[l], mix_w_in[l], conv_w[l], a_log[l], dt_bias[l], tm)
        y_gdn = _gdn(q, k, v, z, bg, gdn_norm_w[l], tl)
        yt_s5 = _s5(ut, s5_a_re[l], s5_a_im[l], s5_log_dt[l], s5_b_re[l], s5_b_im[l],
                    s5_c_re[l], s5_c_im[l], s5_d[l])
        x = _mix_out(x, mod, y_gdn, yt_s5, s5_w_glu[l], mix_w_out[l], mix_norm_post[l], tm_out)
        x = _ffn(x, mod, 2, ff2_norm_pre[l], ff2_norm_post[l], ff2_w_in, ff2_w_out, l, tm_ffn)
    return x
```

```python
import functools
import math

import jax
import jax.numpy as jnp
from jax import lax
from jax.experimental import pallas as pl
from jax.experimental.pallas import tpu as pltpu

F32 = jnp.float32
BF16 = jnp.bfloat16
EPS = 1e-6

HEAD_DIM = 128
GDN_CHUNK = 64
CONV_K = 4
S5_GROUP = 16
S5_STATE = 64
N_MOD = 9
S5_T = 128
BG_LANES = 128

V7X_VMEM_LIMIT_BYTES = 56 * 1024 * 1024
HIGHEST = lax.Precision.HIGHEST


def _params(semantics):
    return pltpu.CompilerParams(dimension_semantics=semantics, vmem_limit_bytes=V7X_VMEM_LIMIT_BYTES)


def _resident(block_shape, index_map):
    return pl.BlockSpec(block_shape, index_map, pipeline_mode=pl.Buffered(1))


def _dot(a, b):
    return jnp.dot(a.astype(BF16), b.astype(BF16), preferred_element_type=F32)


def _dot_nt(a, b):
    return lax.dot_general(a.astype(BF16), b.astype(BF16), (((1,), (1,)), ((), ())),
                           preferred_element_type=F32)


def _dot_tn(a, b):
    return lax.dot_general(a.astype(BF16), b.astype(BF16), (((0,), (0,)), ((), ())),
                           preferred_element_type=F32)


def _dot_split(a, b):
    a_hi = a.astype(BF16)
    a_lo = (a - a_hi.astype(F32)).astype(BF16)
    b_hi = b.astype(BF16)
    b_lo = (b - b_hi.astype(F32)).astype(BF16)
    mm = functools.partial(jnp.dot, preferred_element_type=F32)
    return mm(a_hi, b_hi) + mm(a_hi, b_lo) + mm(a_lo, b_hi)


def _dot_01(sel, x):
    hi = x.astype(BF16)
    rest = x - hi.astype(F32)
    mid = rest.astype(BF16)
    lo = (rest - mid.astype(F32)).astype(BF16)
    mm = functools.partial(jnp.dot, preferred_element_type=F32)
    return mm(sel, hi) + mm(sel, mid) + mm(sel, lo)


def _dot_f32(a, b):
    return jnp.dot(a, b, precision=HIGHEST, preferred_element_type=F32)


def _rms(x, w):
    return x * lax.rsqrt(jnp.mean(x * x, axis=-1, keepdims=True) + EPS) * w


def _silu(x):
    return x * jax.nn.sigmoid(x)


def _div_pow2(x, n):
    assert n & (n - 1) == 0
    return jnp.right_shift(x, n.bit_length() - 1)


def _mod_pow2(x, n):
    assert n & (n - 1) == 0
    return jnp.bitwise_and(x, n - 1)


def _mod_kernel(c_ref, w_ref, b_ref, o_ref):
    c = c_ref[...]
    o_ref[0] = _dot(_silu(c), w_ref[0]) + b_ref[0]


def _modulation(c, w_mod, b_mod):
    depth, d, n = w_mod.shape
    bsz = c.shape[0]
    rows = 8
    assert bsz <= rows
    tn = n // 8
    c_pad = jnp.pad(c, ((0, rows - bsz), (0, 0)))
    return pl.pallas_call(
        _mod_kernel,
        grid=(depth, n // tn),
        in_specs=[
            pl.BlockSpec((rows, d), lambda l, j: (0, 0)),
            pl.BlockSpec((1, d, tn), lambda l, j: (l, 0, j)),
            pl.BlockSpec((1, 1, tn), lambda l, j: (l, 0, j)),
        ],
        out_specs=pl.BlockSpec((1, rows, tn), lambda l, j: (l, 0, j)),
        out_shape=jax.ShapeDtypeStruct((depth, rows, n), F32),
        compiler_params=_params(("arbitrary", "arbitrary")),
        name="adaln_mod",
    )(c_pad, w_mod, b_mod.reshape(depth, 1, n))


def _mod_specs(sub, d):
    return [pl.BlockSpec((8, d), functools.partial(lambda b, i, j: (0, j), j=3 * sub + k)) for k in range(3)]


def _ffn_kernel(x_ref, sh_ref, sc_ref, gt_ref, npre_ref, npost_ref, win_ref, wout_ref, o_ref, acc_ref,
                *, n_chunks, tf):
    b = pl.program_id(0)
    sh = sh_ref[pl.ds(b, 1), :]
    sc = sc_ref[pl.ds(b, 1), :]
    gt = gt_ref[pl.ds(b, 1), :]
    f = wout_ref.shape[0]
    x = x_ref[0]
    h = (_rms(x, npre_ref[...]) * (1.0 + sc) + sh).astype(BF16)
    for ci in range(n_chunks):
        lo, hi = ci * tf, (ci + 1) * tf
        gate = _dot(h, win_ref[:, lo:hi])
        up = _dot(h, win_ref[:, f + lo:f + hi])
        part = _dot(_silu(gate) * up, wout_ref[lo:hi, :])
        if ci == 0:
            acc_ref[...] = part
        else:
            acc_ref[...] += part
    o_ref[0] = x + (0.5 * gt) * _rms(acc_ref[...], npost_ref[...])


def _ffn(x, mod, sub, npre, npost, w_in, w_out, layer, tm):
    bsz, seq, d = x.shape
    f = w_out.shape[1]
    tf = 256
    assert f % tf == 0 and seq % tm == 0
    n = f // tf
    return pl.pallas_call(
        functools.partial(_ffn_kernel, n_chunks=n, tf=tf),
        grid=(bsz, seq // tm),
        in_specs=[pl.BlockSpec((1, tm, d), lambda b, i: (b, i, 0))] + _mod_specs(sub, d) + [
            pl.BlockSpec((1, d), lambda b, i: (0, 0)),
            pl.BlockSpec((1, d), lambda b, i: (0, 0)),
            _resident((None, d, 2 * f), lambda b, i: (layer, 0, 0)),
            _resident((None, f, d), lambda b, i: (layer, 0, 0)),
        ],
        out_specs=pl.BlockSpec((1, tm, d), lambda b, i: (b, i, 0)),
        out_shape=jax.ShapeDtypeStruct(x.shape, F32),
        scratch_shapes=[pltpu.VMEM((tm, d), F32)],
        compiler_params=_params(("arbitrary", "arbitrary")),
        name="ffn",
    )(x, mod, mod, mod, npre.reshape(1, d), npost.reshape(1, d), w_in, w_out)


def _mix_in_kernel(x_ref, sh_ref, sc_ref, npre_ref, w_ref, wut_ref, conv_ref, alog_ref, dtb_ref,
                   q_ref, k_ref, v_ref, z_ref, ut_ref, bg_ref, ext_ref, *, heads, gw):
    b = pl.program_id(0)
    i = pl.program_id(1)
    tm = x_ref.shape[1]
    x = x_ref[0]
    sh = sh_ref[pl.ds(b, 1), :]
    sc = sc_ref[pl.ds(b, 1), :]
    inv_rms = lax.rsqrt(jnp.mean(x * x, axis=-1, keepdims=True) + EPS)
    h = (x * inv_rms * (npre_ref[...] * (1.0 + sc)) + sh).astype(BF16)
    proj = lambda lo, hi: jnp.dot(h, w_ref[:, lo:hi], preferred_element_type=F32)

    @pl.when(i == 0)
    def _():
        ext_ref[0:8, :] = jnp.zeros((8, 3 * gw), F32)

    for blk, ref in enumerate((q_ref, k_ref, v_ref)):
        cols = slice(blk * gw, (blk + 1) * gw)
        pj = proj(blk * gw, (blk + 1) * gw)
        tap = lambda j: conv_ref[j:j + 1, cols]
        body = tap(CONV_K - 1) * pj
        for dist in range(1, CONV_K):
            body = body + tap(CONV_K - 1 - dist) * pltpu.roll(pj, dist, axis=0)
        ext_ref[8:16, cols] = pj[0:8]
        head = tap(0) * ext_ref[pl.ds(8 - (CONV_K - 1), 8), cols]
        for j in range(1, CONV_K):
            head = head + tap(j) * ext_ref[pl.ds(8 - (CONV_K - 1) + j, 8), cols]
        ext_ref[0:8, cols] = pj[tm - 8:tm]
        act = _silu(jnp.concatenate([head, body[8:]], axis=0))
        if ref is v_ref:
            ref[0] = act
        else:
            for hd in range(heads):
                t = act[:, hd * HEAD_DIM:(hd + 1) * HEAD_DIM]
                ref[0, :, hd * HEAD_DIM:(hd + 1) * HEAD_DIM] = t * lax.rsqrt(
                    jnp.sum(t * t, axis=-1, keepdims=True) + EPS)
    z_ref[0] = proj(3 * gw, 4 * gw)

    ut = lax.dot_general(wut_ref[...], h, (((1,), (1,)), ((), ())), preferred_element_type=F32)
    for grp in range(ut_ref.shape[0]):
        for cc in range(tm // S5_T):
            ut_ref[grp, 0, cc] = ut[grp * S5_GROUP:(grp + 1) * S5_GROUP, cc * S5_T:(cc + 1) * S5_T]

    ba = proj(4 * gw, 4 * gw + BG_LANES)
    beta = jax.nn.sigmoid(ba)
    t = ba + dtb_ref[...]
    softplus = jnp.maximum(t, 0.0) + jnp.log1p(jnp.exp(-jnp.abs(t)))
    g = -jnp.exp(alog_ref[...]) * softplus
    lane = lax.broadcasted_iota(jnp.int32, ba.shape, 1)
    bg_ref[0] = jnp.where(lane < heads, beta, g)


def _mix_in(x, mod, npre, w_in, conv_w, a_log, dt_bias, tm):
    bsz, seq, d = x.shape
    gw = d // 2
    sw = d - gw
    heads = gw // HEAD_DIM
    groups = sw // S5_GROUP
    assert 2 * heads <= BG_LANES and seq % tm == 0 and tm % S5_T == 0
    nc = seq // S5_T
    qkv_w, z_w, beta_w, a_w, u_w = jnp.split(
        w_in, [3 * gw, 4 * gw, 4 * gw + heads, 4 * gw + 2 * heads], axis=1)
    pad = jnp.zeros((d, BG_LANES - 2 * heads), w_in.dtype)
    w = jnp.concatenate([qkv_w, z_w, beta_w, a_w, pad], axis=1).astype(BF16)
    wut = u_w.T.astype(BF16)
    n = w.shape[1]
    lane_pad = (0, BG_LANES - 2 * heads)
    alog = jnp.pad(jnp.concatenate([jnp.zeros_like(a_log), a_log]), lane_pad).reshape(1, BG_LANES)
    dtb = jnp.pad(jnp.concatenate([jnp.zeros_like(dt_bias), dt_bias]), lane_pad).reshape(1, BG_LANES)
    tok = lambda width: pl.BlockSpec((1, tm, width), lambda b, i: (b, i, 0))
    shp = lambda width: jax.ShapeDtypeStruct((bsz, seq, width), F32)
    return pl.pallas_call(
        functools.partial(_mix_in_kernel, heads=heads, gw=gw),
        grid=(bsz, seq // tm),
        in_specs=[tok(d)] + _mod_specs(1, d)[:2] + [
            pl.BlockSpec((1, d), lambda b, i: (0, 0)),
            _resident((d, n), lambda b, i: (0, 0)),
            _resident((sw, d), lambda b, i: (0, 0)),
            pl.BlockSpec((CONV_K, 3 * gw), lambda b, i: (0, 0)),
            pl.BlockSpec((1, BG_LANES), lambda b, i: (0, 0)),
            pl.BlockSpec((1, BG_LANES), lambda b, i: (0, 0)),
        ],
        out_specs=[tok(gw), tok(gw), tok(gw), tok(gw),
                   pl.BlockSpec((groups, 1, tm // S5_T, S5_GROUP, S5_T), lambda b, i: (0, b, i, 0, 0)),
                   tok(BG_LANES)],
        out_shape=[shp(gw), shp(gw), shp(gw), shp(gw),
                   jax.ShapeDtypeStruct((groups, bsz, nc, S5_GROUP, S5_T), F32),
                   shp(BG_LANES)],
        scratch_shapes=[pltpu.VMEM((16, 3 * gw), F32)],
        compiler_params=_params(("arbitrary", "arbitrary")),
        name="mix_in",
    )(x, mod, mod, npre.reshape(1, d), w, wut, conv_w, alog, dtb)


def _gdn_kernel(q_ref, k_ref, v_ref, bg_ref, z_ref, nw_ref, y_ref,
                s_ref, sol_ref, attn_ref, ks_ref, qs_ref, tot_ref, *, heads, n_chunks):
    c = GDN_CHUNK
    step = pl.program_id(1)
    slot_w = lax.rem(step, 2)
    slot_r = 1 - slot_w

    @pl.when(step == 0)
    def _():
        s_ref[...] = jnp.zeros_like(s_ref)
        for ref in (sol_ref, attn_ref, ks_ref, qs_ref, tot_ref):
            ref[1] = jnp.zeros(ref.shape[1:], F32)

    row = lax.broadcasted_iota(jnp.int32, (c, c), 0)
    col = lax.broadcasted_iota(jnp.int32, (c, c), 1)
    lower_incl = row >= col
    strict = row > col
    eye = (row == col).astype(F32)
    sum_rows = lax.broadcasted_iota(jnp.int32, (c + HEAD_DIM, c), 0)
    sum_cols = lax.broadcasted_iota(jnp.int32, (c + HEAD_DIM, c), 1)
    cum_and_total = ((sum_rows >= sum_cols) | (sum_rows >= c)).astype(BF16)
    ones_c = jnp.ones((c, c), BF16)
    wide_row = lax.broadcasted_iota(jnp.int32, (c, HEAD_DIM), 0)
    wide_col = lax.broadcasted_iota(jnp.int32, (c, HEAD_DIM), 1)
    utri_wide = ((wide_row <= wide_col) & (wide_col < c)).astype(F32)
    scale = HEAD_DIM ** -0.5
    nw = nw_ref[...]

    pairs = [(ci, hd) for ci in range(n_chunks) for hd in range(heads)]
    rows_of = lambda ci: slice(ci * c, (ci + 1) * c)
    lanes_of = lambda hd: slice(hd * HEAD_DIM, (hd + 1) * HEAD_DIM)
    lane1 = lambda x, j: x[:, j:j + 1]

    pid = {p: n for n, p in enumerate(pairs)}
    t = {}

    def intra_setup():
        bgs = [bg_ref[0, rows_of(ci), :] for ci in range(n_chunks)]
        sums = [_dot_01(cum_and_total, bg) for bg in bgs]
        gc_rows = [_dot_01(ones_c, jnp.concatenate(
            [lane1(bg, heads + hd) * utri_wide for hd in range(heads)], axis=-1)) for bg in bgs]
        for ci in range(n_chunks):
            tot_ref[slot_w, ci] = sums[ci][c:]
        for p in pairs:
            ci, hd = p
            t["beta", p] = lane1(bgs[ci], hd)
            t["gc", p] = lane1(sums[ci][:c], heads + hd)
            gl = lane1(sums[ci][c:], heads + hd)
            gc_row = gc_rows[ci][:, hd * HEAD_DIM:hd * HEAD_DIM + c]
            t["decay", p] = jnp.exp(jnp.where(lower_incl, t["gc", p] - gc_row, -jnp.inf))
            t["q", p] = q_ref[0, rows_of(ci), lanes_of(hd)] * scale
            t["k", p] = k_ref[0, rows_of(ci), lanes_of(hd)]
            t["kb", p] = t["k", p] * t["beta", p]
            t["eg", p] = jnp.exp(t["gc", p])
            ks_ref[slot_w, pid[p]] = t["k", p] * jnp.exp(gl[:c] - t["gc", p])
            qs_ref[slot_w, pid[p]] = t["q", p] * t["eg", p]

    def intra_a_mat():
        for p in pairs:
            t["a", p] = jnp.where(strict, -(_dot_nt(t["kb", p], t["k", p]) * t["decay", p]), 0.0)
            t["pow", p] = t["a", p]
            t["inv", p] = eye + t["a", p]

    def intra_square():
        for p in pairs:
            t["pow", p] = _dot(t["pow", p], t["pow", p])

    def intra_extend():
        for p in pairs:
            t["inv", p] = t["inv", p] + _dot(t["inv", p], t["pow", p])

    def intra_resid():
        for p in pairs:
            t["resid", p] = (eye - t["inv", p]) + _dot_split(t["a", p], t["inv", p])

    def intra_newton():
        for p in pairs:
            t["inv", p] = t["inv", p] + _dot(t["inv", p], t["resid", p])

    def intra_solve():
        for p in pairs:
            ci, hd = p
            rhs = jnp.concatenate([v_ref[0, rows_of(ci), lanes_of(hd)] * t["beta", p],
                                   t["kb", p] * t["eg", p]], axis=-1)
            sol_ref[slot_w, pid[p]] = _dot(t["inv", p], rhs)

    def intra_attn():
        for p in pairs:
            attn_ref[slot_w, pid[p]] = _dot_nt(t["q", p], t["k", p]) * t["decay", p]

    intra = [intra_setup, intra_a_mat]
    for _ in range(int(math.log2(c)) - 2):
        intra += [intra_square, intra_extend]
    intra += [intra_resid, intra_newton, intra_solve, intra_attn]

    r = {"state": [s_ref[hd] for hd in range(heads)]}

    def inter_v_new(ci):
        def run():
            r["sb"] = [r["state"][hd].astype(BF16) for hd in range(heads)]
            r["v_new"] = []
            for hd in range(heads):
                sol = sol_ref[slot_r, pid[ci, hd]]
                r["v_new"].append(sol[:, :HEAD_DIM] - _dot(sol[:, HEAD_DIM:], r["sb"][hd]))
        return run

    def inter_out(ci):
        def run():
            new_state = []
            for hd in range(heads):
                n = pid[ci, hd]
                o = _dot(qs_ref[slot_r, n], r["sb"][hd]) + _dot(attn_ref[slot_r, n], r["v_new"][hd])
                gl = lane1(tot_ref[slot_r, ci], heads + hd)
                new_state.append(r["state"][hd] * jnp.exp(gl) + _dot_tn(ks_ref[slot_r, n], r["v_new"][hd]))
                z = z_ref[0, rows_of(ci), lanes_of(hd)]
                y_ref[0, rows_of(ci), lanes_of(hd)] = (_rms(o, nw) * _silu(z)).astype(BF16)
            r["state"] = new_state
        return run

    inter = []
    for ci in range(n_chunks):
        inter += [inter_v_new(ci), inter_out(ci)]

    for n in range(max(len(intra), len(inter))):
        if n < len(intra):
            intra[n]()
        if n < len(inter):
            inter[n]()
    for hd in range(heads):
        s_ref[hd] = r["state"][hd]


def _gdn(q, k, v, z, bg, norm_w, tl):
    bsz, seq, gw = q.shape
    heads = gw // HEAD_DIM
    assert seq % tl == 0 and tl % GDN_CHUNK == 0
    n_tiles = seq // tl
    n_chunks = tl // GDN_CHUNK
    n_pairs = n_chunks * heads
    cur = lambda width: pl.BlockSpec((1, tl, width), lambda b, i: (b, jnp.minimum(i, n_tiles - 1), 0))
    prev = lambda width: pl.BlockSpec((1, tl, width), lambda b, i: (b, jnp.maximum(i - 1, 0), 0))
    slots = lambda *shape: pltpu.VMEM((2,) + shape, F32)
    return pl.pallas_call(
        functools.partial(_gdn_kernel, heads=heads, n_chunks=n_chunks),
        grid=(bsz, n_tiles + 1),
        in_specs=[cur(gw), cur(gw), cur(gw), cur(BG_LANES), prev(gw),
                  pl.BlockSpec((1, HEAD_DIM), lambda b, i: (0, 0))],
        out_specs=prev(gw),
        out_shape=jax.ShapeDtypeStruct((bsz, seq, gw), BF16),
        scratch_shapes=[pltpu.VMEM((heads, HEAD_DIM, HEAD_DIM), F32),
                        slots(n_pairs, GDN_CHUNK, 2 * HEAD_DIM),
                        slots(n_pairs, GDN_CHUNK, GDN_CHUNK),
                        slots(n_pairs, GDN_CHUNK, HEAD_DIM),
                        slots(n_pairs, GDN_CHUNK, HEAD_DIM),
                        slots(n_chunks, HEAD_DIM, BG_LANES)],
        compiler_params=_params(("arbitrary", "arbitrary")),
        name="gdn",
    )(q, k, v, bg, z, norm_w.reshape(1, HEAD_DIM))


def _s5_param_kernel(are_ref, aim_ref, arc_ref, aic_ref, ldt_ref, brt_ref, bit_ref, cr_ref, ci_ref,
                     crt_ref, cit_ref, kft_ref, p_ref, g1t_ref, mul_ref, *, n_levels):
    t_len = S5_T
    ns = S5_STATE
    hw = S5_GROUP
    dt = jnp.exp(ldt_ref[0])

    ar = jnp.minimum(are_ref[0], -1e-4)
    ai = aim_ref[0]
    mag = jnp.exp(dt * ar)
    abar_re = mag * jnp.cos(dt * ai)
    abar_im = mag * jnp.sin(dt * ai)
    denom = ar * ar + ai * ai
    zr = abar_re - 1.0
    zi = abar_im
    fr = (zr * ar + zi * ai) / denom
    fi = (zi * ar - zr * ai) / denom
    brt = brt_ref[0]
    bit = bit_ref[0]
    bbar_re_t = fr * brt - fi * bit
    bbar_im_t = fr * bit + fi * brt

    tau_rev = (t_len - 1 - lax.broadcasted_iota(jnp.int32, (t_len, ns), 0)).astype(F32)
    rev_mag = jnp.exp(tau_rev * (dt * ar))
    rev_re = rev_mag * jnp.cos(tau_rev * (dt * ai))
    rev_im = rev_mag * jnp.sin(tau_rev * (dt * ai))
    for hp in range(hw):
        b_r = bbar_re_t[hp:hp + 1, :]
        b_i = bbar_im_t[hp:hp + 1, :]
        p_ref[0, hp * t_len:(hp + 1) * t_len, :] = jnp.concatenate(
            [rev_re * b_r - rev_im * b_i, rev_re * b_i + rev_im * b_r], axis=-1).astype(BF16)

    ar_c = jnp.minimum(arc_ref[0], -1e-4)
    ai_c = aic_ref[0]
    tau = lax.broadcasted_iota(jnp.int32, (ns, t_len), 1).astype(F32)

    def lam_pow(t):
        m = jnp.exp(t * (dt * ar_c))
        return m * jnp.cos(t * (dt * ai_c)), m * jnp.sin(t * (dt * ai_c))

    pow_re, pow_im = lam_pow(tau)
    rows = hw * hw
    r_idx = lax.broadcasted_iota(jnp.int32, (rows, hw), 0)
    c_idx = lax.broadcasted_iota(jnp.int32, (rows, hw), 1)
    pick_h = (_mod_pow2(r_idx, hw) == c_idx).astype(F32)
    pick_hp = (_div_pow2(r_idx, hw) == c_idx).astype(F32)
    c_r = _dot_f32(pick_h, cr_ref[0])
    c_i = _dot_f32(pick_h, ci_ref[0])
    b_r = _dot_f32(pick_hp, bbar_re_t)
    b_i = _dot_f32(pick_hp, bbar_im_t)
    cb = jnp.concatenate([c_r * b_r - c_i * b_i, -(c_r * b_i + c_i * b_r)], axis=-1)
    kft_ref[0] = _dot_f32(cb, jnp.concatenate([pow_re, pow_im], axis=0))

    lam_re = pow_re[:, 1:2]
    lam_im = pow_im[:, 1:2]
    nxt_re = pow_re * lam_re - pow_im * lam_im
    nxt_im = pow_re * lam_im + pow_im * lam_re
    crt = crt_ref[0]
    cit = cit_ref[0]
    for h in range(hw):
        c_r = crt[:, h:h + 1]
        c_i = cit[:, h:h + 1]
        g1t_ref[0, :, h * t_len:(h + 1) * t_len] = jnp.concatenate(
            [c_r * nxt_re - c_i * nxt_im, -(c_r * nxt_im + c_i * nxt_re)], axis=0)

    step_mag = jnp.exp(float(t_len) * (dt * ar))
    step_re = step_mag * jnp.cos(float(t_len) * (dt * ai))
    step_im = step_mag * jnp.sin(float(t_len) * (dt * ai))
    for lvl in range(n_levels):
        mul_ref[0, 2 * lvl:2 * lvl + 1, :] = jnp.concatenate([step_re, step_re], axis=-1)
        mul_ref[0, 2 * lvl + 1:2 * lvl + 2, :] = jnp.concatenate([-step_im, step_im], axis=-1)
        step_re, step_im = step_re * step_re - step_im * step_im, 2.0 * step_re * step_im


def _s5_main_kernel(ut_ref, kft_ref, p_ref, g1t_ref, mul_ref, d_ref, yt_ref, m_ref, a_ref,
                    *, n_chunks, n_levels):
    t_len = S5_T
    hw = S5_GROUP
    rows = a_ref.shape[0]

    causal = (lax.broadcasted_iota(jnp.int32, (t_len, t_len), 1)
              >= lax.broadcasted_iota(jnp.int32, (t_len, t_len), 0))
    def toeplitz_rows(hp, carry):
        r0 = pl.multiple_of(hp * t_len, t_len)
        for h in range(hw):
            k_row = kft_ref[0, pl.ds(hp * hw + h, 1), :]
            blk = pltpu.roll(jnp.broadcast_to(k_row, (t_len, t_len)), 0, 1, stride=1, stride_axis=0)
            m_ref[pl.ds(r0, t_len), h * t_len:(h + 1) * t_len] = jnp.where(causal, blk, 0.0).astype(BF16)
        return carry

    lax.fori_loop(0, hw, toeplitz_rows, 0)

    for hp in range(hw):
        a_ref[:, hp * t_len:(hp + 1) * t_len] = ut_ref[0, pl.ds(hp, rows, stride=hw), :]
    a = a_ref[...]
    a_bf = a.astype(BF16)
    state = jnp.dot(a_bf, p_ref[0], preferred_element_type=F32)
    chunk = _mod_pow2(lax.broadcasted_iota(jnp.int32, state.shape, 0), n_chunks)
    for lvl in range(n_levels):
        dist = 2 ** lvl
        prev = jnp.where(chunk >= dist, pltpu.roll(state, dist, axis=0), 0.0)
        state = (state + prev * mul_ref[0, 2 * lvl:2 * lvl + 1, :]
                 + pltpu.roll(prev, S5_STATE, axis=1) * mul_ref[0, 2 * lvl + 1:2 * lvl + 2, :])
    incoming = jnp.where(chunk >= 1, pltpu.roll(state, 1, axis=0), 0.0)
    y = (jnp.dot(a_bf, m_ref[...], preferred_element_type=F32)
         + _dot(incoming, g1t_ref[0]) + d_ref[0] * a)
    y = jax.nn.gelu(y)
    for h in range(hw):
        yt_ref[0, pl.ds(h, rows, stride=hw), :] = y[:, h * t_len:(h + 1) * t_len]


def _s5(ut, a_re, a_im, log_dt, b_re, b_im, c_re, c_im, d_skip):
    groups, bsz, nc, hw, t_len = ut.shape
    n_levels = max(1, int(math.ceil(math.log2(nc))))
    tw = t_len * hw
    ns = S5_STATE
    grp = lambda *tail: pl.BlockSpec((1,) + tail, lambda g: (g,) + (0,) * len(tail))
    col = lambda a: a.reshape(groups, ns, 1)
    row = lambda a: a.reshape(groups, 1, ns)

    kft, p_op, g1t, mul = pl.pallas_call(
        functools.partial(_s5_param_kernel, n_levels=n_levels),
        grid=(groups,),
        in_specs=[grp(1, ns), grp(1, ns), grp(ns, 1), grp(ns, 1), grp(1, 1),
                  grp(hw, ns), grp(hw, ns), grp(hw, ns), grp(hw, ns), grp(ns, hw), grp(ns, hw)],
        out_specs=[grp(hw * hw, t_len), grp(tw, 2 * ns), grp(2 * ns, tw), grp(2 * n_levels, 2 * ns)],
        out_shape=[jax.ShapeDtypeStruct((groups, hw * hw, t_len), F32),
                   jax.ShapeDtypeStruct((groups, tw, 2 * ns), BF16),
                   jax.ShapeDtypeStruct((groups, 2 * ns, tw), F32),
                   jax.ShapeDtypeStruct((groups, 2 * n_levels, 2 * ns), F32)],
        compiler_params=_params(("arbitrary",)),
        name="s5_params",
    )(row(a_re), row(a_im), col(a_re), col(a_im), log_dt.reshape(groups, 1, 1),
      b_re.transpose(0, 2, 1), b_im.transpose(0, 2, 1), c_re, c_im,
      c_re.transpose(0, 2, 1), c_im.transpose(0, 2, 1))

    rows = bsz * nc
    d_row = jnp.repeat(d_skip, t_len, axis=1).reshape(groups, 1, tw)
    yt = pl.pallas_call(
        functools.partial(_s5_main_kernel, n_chunks=nc, n_levels=n_levels),
        grid=(groups,),
        in_specs=[grp(rows * hw, t_len), grp(hw * hw, t_len), grp(tw, 2 * ns), grp(2 * ns, tw),
                  grp(2 * n_levels, 2 * ns), grp(1, tw)],
        out_specs=grp(rows * hw, t_len),
        out_shape=jax.ShapeDtypeStruct((groups, rows * hw, t_len), F32),
        scratch_shapes=[pltpu.VMEM((tw, tw), BF16), pltpu.VMEM((rows, tw), F32)],
        compiler_params=_params(("arbitrary",)),
        name="s5_main",
    )(ut.reshape(groups, rows * hw, t_len), kft, p_op, g1t, mul, d_row)
    return yt.reshape(groups, bsz, nc, hw, t_len)


def _mix_out_kernel(x_ref, gt_ref, yg_ref, yst_ref, wglut_ref, wout_ref, npost_ref, o_ref, yt_ref):
    b = pl.program_id(0)
    gt = gt_ref[pl.ds(b, 1), :]
    for grp in range(yst_ref.shape[0]):
        for cc in range(yst_ref.shape[2]):
            yt_ref[grp * S5_GROUP:(grp + 1) * S5_GROUP, cc * S5_T:(cc + 1) * S5_T] = yst_ref[grp, 0, cc]
    yt = yt_ref[...]
    st = yt * jax.nn.sigmoid(_dot(wglut_ref[...], yt))
    gw = yg_ref.shape[2]
    y = _dot(yg_ref[0], wout_ref[:gw, :]) + _dot_tn(st, wout_ref[gw:, :])
    o_ref[0] = x_ref[0] + gt * _rms(y, npost_ref[...])


def _mix_out(x, mod, y_gdn, yt_s5, w_glu, w_out, npost, tm):
    bsz, seq, d = x.shape
    gw = y_gdn.shape[2]
    groups, _, _, hw, t_len = yt_s5.shape
    sw = groups * hw
    tok = lambda width: pl.BlockSpec((1, tm, width), lambda b, i: (b, i, 0))
    return pl.pallas_call(
        _mix_out_kernel,
        grid=(bsz, seq // tm),
        in_specs=[tok(d), _mod_specs(1, d)[2], tok(gw),
                  pl.BlockSpec((groups, 1, tm // t_len, hw, t_len), lambda b, i: (0, b, i, 0, 0)),
                  _resident((sw, sw), lambda b, i: (0, 0)),
                  _resident((gw + sw, d), lambda b, i: (0, 0)),
                  pl.BlockSpec((1, d), lambda b, i: (0, 0))],
        out_specs=tok(d),
        out_shape=jax.ShapeDtypeStruct(x.shape, F32),
        scratch_shapes=[pltpu.VMEM((sw, tm), F32)],
        compiler_params=_params(("arbitrary", "arbitrary")),
        name="mix_out",
    )(x, mod, y_gdn, yt_s5, w_glu.T.astype(BF16), w_out.astype(BF16), npost.reshape(1, d))


def kernel(x, c, w_mod, b_mod, ff1_norm_pre, ff1_norm_post, ff1_w_in, ff1_w_out, mix_norm_pre, mix_norm_post, mix_w_in, conv_w, a_log, dt_bias, gdn_norm_w, s5_a_re, s5_a_im, s5_log_dt, s5_b_re, s5_b_im, s5_c_re, s5_c_im, s5_d, s5_w_glu, mix_w_out, ff2_norm_pre, ff2_norm_post, ff2_w_in, ff2_w_out):
    depth = w_mod.shape[0]
    seq = x.shape[1]
    tm = min(1024, seq)
    tm_ffn = min(512, seq)
    tm_out = min(1024, seq)
    tl = min(512, seq)
    mods = _modulation(c, w_mod, b_mod)
    for l in range(depth):
        mod = mods[l]
        x = _ffn(x, mod, 0, ff1_norm_pre[l], ff1_norm_post[l], ff1_w_in, ff1_w_out, l, tm_ffn)
        q, k, v, z, ut, bg = _mix_in(x, mod, mix_norm_pre[l], mix_w_in[l], conv_w[l], a_log[l], dt_bias[l], tm)
        y_gdn = _gdn(q, k, v, z, bg, gdn_norm_w[l], tl)
        yt_s5 = _s5(ut, s5_a_re[l], s5_a_im[l], s5_log_dt[l], s5_b_re[l], s5_b_im[l],
                    s5_c_re[l], s5_c_im[l], s5_d[l])
        x = _mix_out(x, mod, y_gdn, yt_s5, s5_w_glu[l], mix_w_out[l], mix_norm_post[l], tm_out)
        x = _ffn(x, mod, 2, ff2_norm_pre[l], ff2_norm_post[l], ff2_w_in, ff2_w_out, l, tm_ffn)
    return x
```

```python
import functools
import math

import jax
import jax.numpy as jnp
from jax import lax
from jax.experimental import pallas as pl
from jax.experimental.pallas import tpu as pltpu

F32 = jnp.float32
BF16 = jnp.bfloat16
EPS = 1e-6

HEAD_DIM = 128
GDN_CHUNK = 64
CONV_K = 4
S5_GROUP = 16
S5_STATE = 64
N_MOD = 9
S5_T = 128
BG_LANES = 128

V7X_VMEM_LIMIT_BYTES = 56 * 1024 * 1024
HIGHEST = lax.Precision.HIGHEST


def _params(semantics):
    return pltpu.CompilerParams(dimension_semantics=semantics, vmem_limit_bytes=V7X_VMEM_LIMIT_BYTES)


def _resident(block_shape, index_map):
    return pl.BlockSpec(block_shape, index_map, pipeline_mode=pl.Buffered(1))


def _dot(a, b):
    return jnp.dot(a.astype(BF16), b.astype(BF16), preferred_element_type=F32)


def _dot_nt(a, b):
    return lax.dot_general(a.astype(BF16), b.astype(BF16), (((1,), (1,)), ((), ())),
                           preferred_element_type=F32)


def _dot_tn(a, b):
    return lax.dot_general(a.astype(BF16), b.astype(BF16), (((0,), (0,)), ((), ())),
                           preferred_element_type=F32)


def _dot_split(a, b):
    a_hi = a.astype(BF16)
    a_lo = (a - a_hi.astype(F32)).astype(BF16)
    b_hi = b.astype(BF16)
    b_lo = (b - b_hi.astype(F32)).astype(BF16)
    mm = functools.partial(jnp.dot, preferred_element_type=F32)
    return mm(a_hi, b_hi) + mm(a_hi, b_lo) + mm(a_lo, b_hi)


def _dot_01(sel, x):
    hi = x.astype(BF16)
    rest = x - hi.astype(F32)
    mid = rest.astype(BF16)
    lo = (rest - mid.astype(F32)).astype(BF16)
    mm = functools.partial(jnp.dot, preferred_element_type=F32)
    return mm(sel, hi) + mm(sel, mid) + mm(sel, lo)


def _dot_f32(a, b):
    return jnp.dot(a, b, precision=HIGHEST, preferred_element_type=F32)


def _rms(x, w):
    return x * lax.rsqrt(jnp.mean(x * x, axis=-1, keepdims=True) + EPS) * w


def _silu(x):
    return x * jax.nn.sigmoid(x)


def _div_pow2(x, n):
    assert n & (n - 1) == 0
    return jnp.right_shift(x, n.bit_length() - 1)


def _mod_pow2(x, n):
    assert n & (n - 1) == 0
    return jnp.bitwise_and(x, n - 1)


def _mod_kernel(c_ref, w_ref, b_ref, o_ref):
    c = c_ref[...]
    o_ref[0] = _dot(_silu(c), w_ref[0]) + b_ref[0]


def _modulation(c, w_mod, b_mod):
    depth, d, n = w_mod.shape
    bsz = c.shape[0]
    rows = 8
    assert bsz <= rows
    tn = n // 8
    c_pad = jnp.pad(c, ((0, rows - bsz), (0, 0)))
    return pl.pallas_call(
        _mod_kernel,
        grid=(depth, n // tn),
        in_specs=[
            pl.BlockSpec((rows, d), lambda l, j: (0, 0)),
            pl.BlockSpec((1, d, tn), lambda l, j: (l, 0, j)),
            pl.BlockSpec((1, 1, tn), lambda l, j: (l, 0, j)),
        ],
        out_specs=pl.BlockSpec((1, rows, tn), lambda l, j: (l, 0, j)),
        out_shape=jax.ShapeDtypeStruct((depth, rows, n), F32),
        compiler_params=_params(("arbitrary", "arbitrary")),
        name="adaln_mod",
    )(c_pad, w_mod, b_mod.reshape(depth, 1, n))


def _mod_specs(sub, d):
    return [pl.BlockSpec((8, d), functools.partial(lambda b, i, j: (0, j), j=3 * sub + k)) for k in range(3)]


def _ffn_kernel(x_ref, sh_ref, sc_ref, gt_ref, npre_ref, npost_ref, win_ref, wout_ref, o_ref, acc_ref,
                *, n_chunks, tf):
    b = pl.program_id(0)
    sh = sh_ref[pl.ds(b, 1), :]
    sc = sc_ref[pl.ds(b, 1), :]
    gt = gt_ref[pl.ds(b, 1), :]
    f = wout_ref.shape[0]
    x = x_ref[0]
    h = (_rms(x, npre_ref[...]) * (1.0 + sc) + sh).astype(BF16)
    for ci in range(n_chunks):
        lo, hi = ci * tf, (ci + 1) * tf
        gate = _dot(h, win_ref[:, lo:hi])
        up = _dot(h, win_ref[:, f + lo:f + hi])
        part = _dot(_silu(gate) * up, wout_ref[lo:hi, :])
        if ci == 0:
            acc_ref[...] = part
        else:
            acc_ref[...] += part
    o_ref[0] = x + (0.5 * gt) * _rms(acc_ref[...], npost_ref[...])


def _ffn(x, mod, sub, npre, npost, w_in, w_out, layer, tm):
    bsz, seq, d = x.shape
    f = w_out.shape[1]
    tf = 256
    assert f % tf == 0 and seq % tm == 0
    n = f // tf
    return pl.pallas_call(
        functools.partial(_ffn_kernel, n_chunks=n, tf=tf),
        grid=(bsz, seq // tm),
        in_specs=[pl.BlockSpec((1, tm, d), lambda b, i: (b, i, 0))] + _mod_specs(sub, d) + [
            pl.BlockSpec((1, d), lambda b, i: (0, 0)),
            pl.BlockSpec((1, d), lambda b, i: (0, 0)),
            _resident((None, d, 2 * f), lambda b, i: (layer, 0, 0)),
            _resident((None, f, d), lambda b, i: (layer, 0, 0)),
        ],
        out_specs=pl.BlockSpec((1, tm, d), lambda b, i: (b, i, 0)),
        out_shape=jax.ShapeDtypeStruct(x.shape, F32),
        scratch_shapes=[pltpu.VMEM((tm, d), F32)],
        compiler_params=_params(("arbitrary", "arbitrary")),
        name="ffn",
    )(x, mod, mod, mod, npre.reshape(1, d), npost.reshape(1, d), w_in, w_out)


def _mix_in_kernel(x_ref, sh_ref, sc_ref, npre_ref, w_ref, wut_ref, conv_ref, alog_ref, dtb_ref,
                   q_ref, k_ref, v_ref, z_ref, ut_ref, bg_ref, ext_ref, *, heads, gw):
    b = pl.program_id(0)
    i = pl.program_id(1)
    tm = x_ref.shape[1]
    x = x_ref[0]
    sh = sh_ref[pl.ds(b, 1), :]
    sc = sc_ref[pl.ds(b, 1), :]
    inv_rms = lax.rsqrt(jnp.mean(x * x, axis=-1, keepdims=True) + EPS)
    h = (x * inv_rms * (npre_ref[...] * (1.0 + sc)) + sh).astype(BF16)
    proj = lambda lo, hi: jnp.dot(h, w_ref[:, lo:hi], preferred_element_type=F32)

    @pl.when(i == 0)
    def _():
        ext_ref[0:8, :] = jnp.zeros((8, 3 * gw), F32)

    for blk, ref in enumerate((q_ref, k_ref, v_ref)):
        cols = slice(blk * gw, (blk + 1) * gw)
        pj = proj(blk * gw, (blk + 1) * gw)
        tap = lambda j: conv_ref[j:j + 1, cols]
        body = tap(CONV_K - 1) * pj
        for dist in range(1, CONV_K):
            body = body + tap(CONV_K - 1 - dist) * pltpu.roll(pj, dist, axis=0)
        ext_ref[8:16, cols] = pj[0:8]
        head = tap(0) * ext_ref[pl.ds(8 - (CONV_K - 1), 8), cols]
        for j in range(1, CONV_K):
            head = head + tap(j) * ext_ref[pl.ds(8 - (CONV_K - 1) + j, 8), cols]
        ext_ref[0:8, cols] = pj[tm - 8:tm]
        act = _silu(jnp.concatenate([head, body[8:]], axis=0))
        if ref is v_ref:
            ref[0] = act
        else:
            for hd in range(heads):
                t = act[:, hd * HEAD_DIM:(hd + 1) * HEAD_DIM]
                ref[0, :, hd * HEAD_DIM:(hd + 1) * HEAD_DIM] = t * lax.rsqrt(
                    jnp.sum(t * t, axis=-1, keepdims=True) + EPS)
    z_ref[0] = proj(3 * gw, 4 * gw)

    ut = lax.dot_general(wut_ref[...], h, (((1,), (1,)), ((), ())), preferred_element_type=F32)
    for grp in range(ut_ref.shape[0]):
        for cc in range(tm // S5_T):
            ut_ref[grp, 0, cc] = ut[grp * S5_GROUP:(grp + 1) * S5_GROUP, cc * S5_T:(cc + 1) * S5_T]

    ba = proj(4 * gw, 4 * gw + BG_LANES)
    beta = jax.nn.sigmoid(ba)
    t = ba + dtb_ref[...]
    softplus = jnp.maximum(t, 0.0) + jnp.log1p(jnp.exp(-jnp.abs(t)))
    g = -jnp.exp(alog_ref[...]) * softplus
    lane = lax.broadcasted_iota(jnp.int32, ba.shape, 1)
    bg_ref[0] = jnp.where(lane < heads, beta, g)


def _mix_in(x, mod, npre, w_in, conv_w, a_log, dt_bias, tm):
    bsz, seq, d = x.shape
    gw = d // 2
    sw = d - gw
    heads = gw // HEAD_DIM
    groups = sw // S5_GROUP
    assert 2 * heads <= BG_LANES and seq % tm == 0 and tm % S5_T == 0
    nc = seq // S5_T
    qkv_w, z_w, beta_w, a_w, u_w = jnp.split(
        w_in, [3 * gw, 4 * gw, 4 * gw + heads, 4 * gw + 2 * heads], axis=1)
    pad = jnp.zeros((d, BG_LANES - 2 * heads), w_in.dtype)
    w = jnp.concatenate([qkv_w, z_w, beta_w, a_w, pad], axis=1).astype(BF16)
    wut = u_w.T.astype(BF16)
    n = w.shape[1]
    lane_pad = (0, BG_LANES - 2 * heads)
    alog = jnp.pad(jnp.concatenate([jnp.zeros_like(a_log), a_log]), lane_pad).reshape(1, BG_LANES)
    dtb = jnp.pad(jnp.concatenate([jnp.zeros_like(dt_bias), dt_bias]), lane_pad).reshape(1, BG_LANES)
    tok = lambda width: pl.BlockSpec((1, tm, width), lambda b, i: (b, i, 0))
    shp = lambda width: jax.ShapeDtypeStruct((bsz, seq, width), F32)
    return pl.pallas_call(
        functools.partial(_mix_in_kernel, heads=heads, gw=gw),
        grid=(bsz, seq // tm),
        in_specs=[tok(d)] + _mod_specs(1, d)[:2] + [
            pl.BlockSpec((1, d), lambda b, i: (0, 0)),
            _resident((d, n), lambda b, i: (0, 0)),
            _resident((sw, d), lambda b, i: (0, 0)),
            pl.BlockSpec((CONV_K, 3 * gw), lambda b, i: (0, 0)),
            pl.BlockSpec((1, BG_LANES), lambda b, i: (0, 0)),
            pl.BlockSpec((1, BG_LANES), lambda b, i: (0, 0)),
        ],
        out_specs=[tok(gw), tok(gw), tok(gw), tok(gw),
                   pl.BlockSpec((groups, 1, tm // S5_T, S5_GROUP, S5_T), lambda b, i: (0, b, i, 0, 0)),
                   tok(BG_LANES)],
        out_shape=[shp(gw), shp(gw), shp(gw), shp(gw),
                   jax.ShapeDtypeStruct((groups, bsz, nc, S5_GROUP, S5_T), F32),
                   shp(BG_LANES)],
        scratch_shapes=[pltpu.VMEM((16, 3 * gw), F32)],
        compiler_params=_params(("arbitrary", "arbitrary")),
        name="mix_in",
    )(x, mod, mod, npre.reshape(1, d), w, wut, conv_w, alog, dtb)


def _gdn_kernel(q_ref, k_ref, v_ref, bg_ref, z_ref, nw_ref, y_ref,
                s_ref, sol_ref, attn_ref, ks_ref, qs_ref, tot_ref, *, heads, n_chunks):
    c = GDN_CHUNK
    step = pl.program_id(1)
    slot_w = lax.rem(step, 2)
    slot_r = 1 - slot_w

    @pl.when(step == 0)
    def _():
        s_ref[...] = jnp.zeros_like(s_ref)
        for ref in (sol_ref, attn_ref, ks_ref, qs_ref, tot_ref):
            ref[1] = jnp.zeros(ref.shape[1:], F32)

    row = lax.broadcasted_iota(jnp.int32, (c, c), 0)
    col = lax.broadcasted_iota(jnp.int32, (c, c), 1)
    lower_incl = row >= col
    strict = row > col
    eye = (row == col).astype(F32)
    sum_rows = lax.broadcasted_iota(jnp.int32, (c + HEAD_DIM, c), 0)
    sum_cols = lax.broadcasted_iota(jnp.int32, (c + HEAD_DIM, c), 1)
    cum_and_total = ((sum_rows >= sum_cols) | (sum_rows >= c)).astype(BF16)
    ones_c = jnp.ones((c, c), BF16)
    wide_row = lax.broadcasted_iota(jnp.int32, (c, HEAD_DIM), 0)
    wide_col = lax.broadcasted_iota(jnp.int32, (c, HEAD_DIM), 1)
    utri_wide = ((wide_row <= wide_col) & (wide_col < c)).astype(F32)
    scale = HEAD_DIM ** -0.5
    nw = nw_ref[...]

    pairs = [(ci, hd) for ci in range(n_chunks) for hd in range(heads)]
    rows_of = lambda ci: slice(ci * c, (ci + 1) * c)
    lanes_of = lambda hd: slice(hd * HEAD_DIM, (hd + 1) * HEAD_DIM)
    lane1 = lambda x, j: x[:, j:j + 1]

    pid = {p: n for n, p in enumerate(pairs)}
    t = {}

    def intra_setup():
        bgs = [bg_ref[0, rows_of(ci), :] for ci in range(n_chunks)]
        sums = [_dot_01(cum_and_total, bg) for bg in bgs]
        gc_rows = [_dot_01(ones_c, jnp.concatenate(
            [lane1(bg, heads + hd) * utri_wide for hd in range(heads)], axis=-1)) for bg in bgs]
        for ci in range(n_chunks):
            tot_ref[slot_w, ci] = sums[ci][c:]
        for p in pairs:
            ci, hd = p
            t["beta", p] = lane1(bgs[ci], hd)
            t["gc", p] = lane1(sums[ci][:c], heads + hd)
            gl = lane1(sums[ci][c:], heads + hd)
            gc_row = gc_rows[ci][:, hd * HEAD_DIM:hd * HEAD_DIM + c]
            t["decay", p] = jnp.exp(jnp.where(lower_incl, t["gc", p] - gc_row, -jnp.inf))
            t["q", p] = q_ref[0, rows_of(ci), lanes_of(hd)] * scale
            t["k", p] = k_ref[0, rows_of(ci), lanes_of(hd)]
            t["kb", p] = t["k", p] * t["beta", p]
            t["eg", p] = jnp.exp(t["gc", p])
            ks_ref[slot_w, pid[p]] = t["k", p] * jnp.exp(gl[:c] - t["gc", p])
            qs_ref[slot_w, pid[p]] = t["q", p] * t["eg", p]

    def intra_a_mat():
        for p in pairs:
            t["a", p] = jnp.where(strict, -(_dot_nt(t["kb", p], t["k", p]) * t["decay", p]), 0.0)
            t["pow", p] = t["a", p]
            t["inv", p] = eye + t["a", p]

    def intra_square():
        for p in pairs:
            t["pow", p] = _dot(t["pow", p], t["pow", p])

    def intra_extend():
        for p in pairs:
            t["inv", p] = t["inv", p] + _dot(t["inv", p], t["pow", p])

    def intra_resid():
        for p in pairs:
            t["resid", p] = (eye - t["inv", p]) + _dot_split(t["a", p], t["inv", p])

    def intra_newton():
        for p in pairs:
            t["inv", p] = t["inv", p] + _dot(t["inv", p], t["resid", p])

    def intra_solve():
        for p in pairs:
            ci, hd = p
            rhs = jnp.concatenate([v_ref[0, rows_of(ci), lanes_of(hd)] * t["beta", p],
                                   t["kb", p] * t["eg", p]], axis=-1)
            sol_ref[slot_w, pid[p]] = _dot(t["inv", p], rhs)

    def intra_attn():
        for p in pairs:
            attn_ref[slot_w, pid[p]] = _dot_nt(t["q", p], t["k", p]) * t["decay", p]

    intra = [intra_setup, intra_a_mat]
    for _ in range(int(math.log2(c)) - 2):
        intra += [intra_square, intra_extend]
    intra += [intra_resid, intra_newton, intra_solve, intra_attn]

    r = {"state": [s_ref[hd] for hd in range(heads)]}

    def inter_v_new(ci):
        def run():
            r["sb"] = [r["state"][hd].astype(BF16) for hd in range(heads)]
            r["v_new"] = []
            for hd in range(heads):
                sol = sol_ref[slot_r, pid[ci, hd]]
                r["v_new"].append(sol[:, :HEAD_DIM] - _dot(sol[:, HEAD_DIM:], r["sb"][hd]))
        return run

    def inter_out(ci):
        def run():
            new_state = []
            for hd in range(heads):
                n = pid[ci, hd]
                o = _dot(qs_ref[slot_r, n], r["sb"][hd]) + _dot(attn_ref[slot_r, n], r["v_new"][hd])
                gl = lane1(tot_ref[slot_r, ci], heads + hd)
                new_state.append(r["state"][hd] * jnp.exp(gl) + _dot_tn(ks_ref[slot_r, n], r["v_new"][hd]))
                z = z_ref[0, rows_of(ci), lanes_of(hd)]
                y_ref[0, rows_of(ci), lanes_of(hd)] = (_rms(o, nw) * _silu(z)).astype(BF16)
            r["state"] = new_state
        return run

    inter = []
    for ci in range(n_chunks):
        inter += [inter_v_new(ci), inter_out(ci)]

    for n in range(max(len(intra), len(inter))):
        if n < len(intra):
            intra[n]()
        if n < len(inter):
            inter[n]()
    for hd in range(heads):
        s_ref[hd] = r["state"][hd]


def _gdn(q, k, v, z, bg, norm_w, tl):
    bsz, seq, gw = q.shape
    heads = gw // HEAD_DIM
    assert seq % tl == 0 and tl % GDN_CHUNK == 0
    n_tiles = seq // tl
    n_chunks = tl // GDN_CHUNK
    n_pairs = n_chunks * heads
    cur = lambda width: pl.BlockSpec((1, tl, width), lambda b, i: (b, jnp.minimum(i, n_tiles - 1), 0))
    prev = lambda width: pl.BlockSpec((1, tl, width), lambda b, i: (b, jnp.maximum(i - 1, 0), 0))
    slots = lambda *shape: pltpu.VMEM((2,) + shape, F32)
    return pl.pallas_call(
        functools.partial(_gdn_kernel, heads=heads, n_chunks=n_chunks),
        grid=(bsz, n_tiles + 1),
        in_specs=[cur(gw), cur(gw), cur(gw), cur(BG_LANES), prev(gw),
                  pl.BlockSpec((1, HEAD_DIM), lambda b, i: (0, 0))],
        out_specs=prev(gw),
        out_shape=jax.ShapeDtypeStruct((bsz, seq, gw), BF16),
        scratch_shapes=[pltpu.VMEM((heads, HEAD_DIM, HEAD_DIM), F32),
                        slots(n_pairs, GDN_CHUNK, 2 * HEAD_DIM),
                        slots(n_pairs, GDN_CHUNK, GDN_CHUNK),
                        slots(n_pairs, GDN_CHUNK, HEAD_DIM),
                        slots(n_pairs, GDN_CHUNK, HEAD_DIM),
                        slots(n_chunks, HEAD_DIM, BG_LANES)],
        compiler_params=_params(("arbitrary", "arbitrary")),
        name="gdn",
    )(q, k, v, bg, z, norm_w.reshape(1, HEAD_DIM))


def _s5_param_kernel(are_ref, aim_ref, arc_ref, aic_ref, ldt_ref, brt_ref, bit_ref, cr_ref, ci_ref,
                     crt_ref, cit_ref, kft_ref, p_ref, g1t_ref, mul_ref, *, n_levels):
    t_len = S5_T
    ns = S5_STATE
    hw = S5_GROUP
    dt = jnp.exp(ldt_ref[0])

    ar = jnp.minimum(are_ref[0], -1e-4)
    ai = aim_ref[0]
    mag = jnp.exp(dt * ar)
    abar_re = mag * jnp.cos(dt * ai)
    abar_im = mag * jnp.sin(dt * ai)
    denom = ar * ar + ai * ai
    zr = abar_re - 1.0
    zi = abar_im
    fr = (zr * ar + zi * ai) / denom
    fi = (zi * ar - zr * ai) / denom
    brt = brt_ref[0]
    bit = bit_ref[0]
    bbar_re_t = fr * brt - fi * bit
    bbar_im_t = fr * bit + fi * brt

    tau_rev = (t_len - 1 - lax.broadcasted_iota(jnp.int32, (t_len, ns), 0)).astype(F32)
    rev_mag = jnp.exp(tau_rev * (dt * ar))
    rev_re = rev_mag * jnp.cos(tau_rev * (dt * ai))
    rev_im = rev_mag * jnp.sin(tau_rev * (dt * ai))
    for hp in range(hw):
        b_r = bbar_re_t[hp:hp + 1, :]
        b_i = bbar_im_t[hp:hp + 1, :]
        p_ref[0, hp * t_len:(hp + 1) * t_len, :] = jnp.concatenate(
            [rev_re * b_r - rev_im * b_i, rev_re * b_i + rev_im * b_r], axis=-1).astype(BF16)

    ar_c = jnp.minimum(arc_ref[0], -1e-4)
    ai_c = aic_ref[0]
    tau = lax.broadcasted_iota(jnp.int32, (ns, t_len), 1).astype(F32)

    def lam_pow(t):
        m = jnp.exp(t * (dt * ar_c))
        return m * jnp.cos(t * (dt * ai_c)), m * jnp.sin(t * (dt * ai_c))

    pow_re, pow_im = lam_pow(tau)
    rows = hw * hw
    r_idx = lax.broadcasted_iota(jnp.int32, (rows, hw), 0)
    c_idx = lax.broadcasted_iota(jnp.int32, (rows, hw), 1)
    pick_h = (_mod_pow2(r_idx, hw) == c_idx).astype(F32)
    pick_hp = (_div_pow2(r_idx, hw) == c_idx).astype(F32)
    c_r = _dot_f32(pick_h, cr_ref[0])
    c_i = _dot_f32(pick_h, ci_ref[0])
    b_r = _dot_f32(pick_hp, bbar_re_t)
    b_i = _dot_f32(pick_hp, bbar_im_t)
    cb = jnp.concatenate([c_r * b_r - c_i * b_i, -(c_r * b_i + c_i * b_r)], axis=-1)
    kft_ref[0] = _dot_f32(cb, jnp.concatenate([pow_re, pow_im], axis=0))

    lam_re = pow_re[:, 1:2]
    lam_im = pow_im[:, 1:2]
    nxt_re = pow_re * lam_re - pow_im * lam_im
    nxt_im = pow_re * lam_im + pow_im * lam_re
    crt = crt_ref[0]
    cit = cit_ref[0]
    for h in range(hw):
        c_r = crt[:, h:h + 1]
        c_i = cit[:, h:h + 1]
        g1t_ref[0, :, h * t_len:(h + 1) * t_len] = jnp.concatenate(
            [c_r * nxt_re - c_i * nxt_im, -(c_r * nxt_im + c_i * nxt_re)], axis=0)

    step_mag = jnp.exp(float(t_len) * (dt * ar))
    step_re = step_mag * jnp.cos(float(t_len) * (dt * ai))
    step_im = step_mag * jnp.sin(float(t_len) * (dt * ai))
    for lvl in range(n_levels):
        mul_ref[0, 2 * lvl:2 * lvl + 1, :] = jnp.concatenate([step_re, step_re], axis=-1)
        mul_ref[0, 2 * lvl + 1:2 * lvl + 2, :] = jnp.concatenate([-step_im, step_im], axis=-1)
        step_re, step_im = step_re * step_re - step_im * step_im, 2.0 * step_re * step_im


def _s5_main_kernel(ut_ref, kft_ref, p_ref, g1t_ref, mul_ref, d_ref, yt_ref, m_ref, a_ref, abf_ref, acc_ref,
                    *, n_chunks, n_levels):
    t_len = S5_T
    hw = S5_GROUP
    rows = a_ref.shape[0]

    causal = (lax.broadcasted_iota(jnp.int32, (t_len, t_len), 1)
              >= lax.broadcasted_iota(jnp.int32, (t_len, t_len), 0))
    strip = 2 * t_len
    n_strips = hw // 2

    def build_strip(k):
        for j in range(2):
            hp = 2 * k + j
            r0 = pl.multiple_of(hp * t_len, t_len)
            for h in range(hw):
                k_row = kft_ref[0, pl.ds(hp * hw + h, 1), :]
                blk = pltpu.roll(jnp.broadcast_to(k_row, (t_len, t_len)), 0, 1, stride=1, stride_axis=0)
                m_ref[pl.ds(r0, t_len), h * t_len:(h + 1) * t_len] = jnp.where(causal, blk, 0.0).astype(BF16)

    for hp in range(hw):
        a_ref[:, hp * t_len:(hp + 1) * t_len] = ut_ref[0, pl.ds(hp, rows, stride=hw), :]
    a = a_ref[...]
    abf_ref[...] = a.astype(BF16)
    state = jnp.dot(abf_ref[...], p_ref[0], preferred_element_type=F32)
    chunk = _mod_pow2(lax.broadcasted_iota(jnp.int32, state.shape, 0), n_chunks)
    for lvl in range(n_levels):
        dist = 2 ** lvl
        prev = jnp.where(chunk >= dist, pltpu.roll(state, dist, axis=0), 0.0)
        state = (state + prev * mul_ref[0, 2 * lvl:2 * lvl + 1, :]
                 + pltpu.roll(prev, S5_STATE, axis=1) * mul_ref[0, 2 * lvl + 1:2 * lvl + 2, :])
    incoming = jnp.where(chunk >= 1, pltpu.roll(state, 1, axis=0), 0.0)
    acc_ref[...] = _dot(incoming, g1t_ref[0]) + d_ref[0] * a

    build_strip(0)

    def contract(k, carry):
        r0 = pl.multiple_of(k * strip, strip)
        acc_ref[...] += jnp.dot(abf_ref[:, pl.ds(r0, strip)], m_ref[pl.ds(r0, strip), :],
                                preferred_element_type=F32)
        build_strip(jnp.minimum(k + 1, n_strips - 1))
        return carry

    lax.fori_loop(0, n_strips, contract, 0)
    y = jax.nn.gelu(acc_ref[...])
    for h in range(hw):
        yt_ref[0, pl.ds(h, rows, stride=hw), :] = y[:, h * t_len:(h + 1) * t_len]


def _s5(ut, a_re, a_im, log_dt, b_re, b_im, c_re, c_im, d_skip):
    groups, bsz, nc, hw, t_len = ut.shape
    n_levels = max(1, int(math.ceil(math.log2(nc))))
    tw = t_len * hw
    ns = S5_STATE
    grp = lambda *tail: pl.BlockSpec((1,) + tail, lambda g: (g,) + (0,) * len(tail))
    col = lambda a: a.reshape(groups, ns, 1)
    row = lambda a: a.reshape(groups, 1, ns)

    kft, p_op, g1t, mul = pl.pallas_call(
        functools.partial(_s5_param_kernel, n_levels=n_levels),
        grid=(groups,),
        in_specs=[grp(1, ns), grp(1, ns), grp(ns, 1), grp(ns, 1), grp(1, 1),
                  grp(hw, ns), grp(hw, ns), grp(hw, ns), grp(hw, ns), grp(ns, hw), grp(ns, hw)],
        out_specs=[grp(hw * hw, t_len), grp(tw, 2 * ns), grp(2 * ns, tw), grp(2 * n_levels, 2 * ns)],
        out_shape=[jax.ShapeDtypeStruct((groups, hw * hw, t_len), F32),
                   jax.ShapeDtypeStruct((groups, tw, 2 * ns), BF16),
                   jax.ShapeDtypeStruct((groups, 2 * ns, tw), F32),
                   jax.ShapeDtypeStruct((groups, 2 * n_levels, 2 * ns), F32)],
        compiler_params=_params(("arbitrary",)),
        name="s5_params",
    )(row(a_re), row(a_im), col(a_re), col(a_im), log_dt.reshape(groups, 1, 1),
      b_re.transpose(0, 2, 1), b_im.transpose(0, 2, 1), c_re, c_im,
      c_re.transpose(0, 2, 1), c_im.transpose(0, 2, 1))

    rows = bsz * nc
    d_row = jnp.repeat(d_skip, t_len, axis=1).reshape(groups, 1, tw)
    yt = pl.pallas_call(
        functools.partial(_s5_main_kernel, n_chunks=nc, n_levels=n_levels),
        grid=(groups,),
        in_specs=[grp(rows * hw, t_len), grp(hw * hw, t_len), grp(tw, 2 * ns), grp(2 * ns, tw),
                  grp(2 * n_levels, 2 * ns), grp(1, tw)],
        out_specs=grp(rows * hw, t_len),
        out_shape=jax.ShapeDtypeStruct((groups, rows * hw, t_len), F32),
        scratch_shapes=[pltpu.VMEM((tw, tw), BF16), pltpu.VMEM((rows, tw), F32),
                        pltpu.VMEM((rows, tw), BF16), pltpu.VMEM((rows, tw), F32)],
        compiler_params=_params(("arbitrary",)),
        name="s5_main",
    )(ut.reshape(groups, rows * hw, t_len), kft, p_op, g1t, mul, d_row)
    return yt.reshape(groups, bsz, nc, hw, t_len)


def _mix_out_kernel(x_ref, gt_ref, yg_ref, yst_ref, wglut_ref, wout_ref, npost_ref, o_ref, yt_ref):
    b = pl.program_id(0)
    gt = gt_ref[pl.ds(b, 1), :]
    for grp in range(yst_ref.shape[0]):
        for cc in range(yst_ref.shape[2]):
            yt_ref[grp * S5_GROUP:(grp + 1) * S5_GROUP, cc * S5_T:(cc + 1) * S5_T] = yst_ref[grp, 0, cc]
    yt = yt_ref[...]
    st = yt * jax.nn.sigmoid(_dot(wglut_ref[...], yt))
    gw = yg_ref.shape[2]
    y = _dot(yg_ref[0], wout_ref[:gw, :]) + _dot_tn(st, wout_ref[gw:, :])
    o_ref[0] = x_ref[0] + gt * _rms(y, npost_ref[...])


def _mix_out(x, mod, y_gdn, yt_s5, w_glu, w_out, npost, tm):
    bsz, seq, d = x.shape
    gw = y_gdn.shape[2]
    groups, _, _, hw, t_len = yt_s5.shape
    sw = groups * hw
    tok = lambda width: pl.BlockSpec((1, tm, width), lambda b, i: (b, i, 0))
    return pl.pallas_call(
        _mix_out_kernel,
        grid=(bsz, seq // tm),
        in_specs=[tok(d), _mod_specs(1, d)[2], tok(gw),
                  pl.BlockSpec((groups, 1, tm // t_len, hw, t_len), lambda b, i: (0, b, i, 0, 0)),
                  _resident((sw, sw), lambda b, i: (0, 0)),
                  _resident((gw + sw, d), lambda b, i: (0, 0)),
                  pl.BlockSpec((1, d), lambda b, i: (0, 0))],
        out_specs=tok(d),
        out_shape=jax.ShapeDtypeStruct(x.shape, F32),
        scratch_shapes=[pltpu.VMEM((sw, tm), F32)],
        compiler_params=_params(("arbitrary", "arbitrary")),
        name="mix_out",
    )(x, mod, y_gdn, yt_s5, w_glu.T.astype(BF16), w_out.astype(BF16), npost.reshape(1, d))


def kernel(x, c, w_mod, b_mod, ff1_norm_pre, ff1_norm_post, ff1_w_in, ff1_w_out, mix_norm_pre, mix_norm_post, mix_w_in, conv_w, a_log, dt_bias, gdn_norm_w, s5_a_re, s5_a_im, s5_log_dt, s5_b_re, s5_b_im, s5_c_re, s5_c_im, s5_d, s5_w_glu, mix_w_out, ff2_norm_pre, ff2_norm_post, ff2_w_in, ff2_w_out):
    depth = w_mod.shape[0]
    seq = x.shape[1]
    tm = min(1024, seq)
    tm_ffn = min(512, seq)
    tm_out = min(1024, seq)
    tl = min(512, seq)
    mods = _modulation(c, w_mod, b_mod)
    for l in range(depth):
        mod = mods[l]
        x = _ffn(x, mod, 0, ff1_norm_pre[l], ff1_norm_post[l], ff1_w_in, ff1_w_out, l, tm_ffn)
        q, k, v, z, ut, bg = _mix_in(x, mod, mix_norm_pre[l], mix_w_in[l], conv_w[l], a_log[l], dt_bias[l], tm)
        y_gdn = _gdn(q, k, v, z, bg, gdn_norm_w[l], tl)
        yt_s5 = _s5(ut, s5_a_re[l], s5_a_im[l], s5_log_dt[l], s5_b_re[l], s5_b_im[l],
                    s5_c_re[l], s5_c_im[l], s5_d[l])
        x = _mix_out(x, mod, y_gdn, yt_s5, s5_w_glu[l], mix_w_out[l], mix_norm_post[l], tm_out)
        x = _ffn(x, mod, 2, ff2_norm_pre[l], ff2_norm_post[l], ff2_w_in, ff2_w_out, l, tm_ffn)
    return x
```

```python
import functools
import math

import jax
import jax.numpy as jnp
from jax import lax
from jax.experimental import pallas as pl
from jax.experimental.pallas import tpu as pltpu

F32 = jnp.float32
BF16 = jnp.bfloat16
EPS = 1e-6

HEAD_DIM = 128
GDN_CHUNK = 64
CONV_K = 4
S5_GROUP = 16
S5_STATE = 64
N_MOD = 9
S5_T = 128
BG_LANES = 128

V7X_VMEM_LIMIT_BYTES = 56 * 1024 * 1024
HIGHEST = lax.Precision.HIGHEST


def _params(semantics):
    return pltpu.CompilerParams(dimension_semantics=semantics, vmem_limit_bytes=V7X_VMEM_LIMIT_BYTES)


def _resident(block_shape, index_map):
    return pl.BlockSpec(block_shape, index_map, pipeline_mode=pl.Buffered(1))


def _dot(a, b):
    return jnp.dot(a.astype(BF16), b.astype(BF16), preferred_element_type=F32)


def _dot_nt(a, b):
    return lax.dot_general(a.astype(BF16), b.astype(BF16), (((1,), (1,)), ((), ())),
                           preferred_element_type=F32)


def _dot_tn(a, b):
    return lax.dot_general(a.astype(BF16), b.astype(BF16), (((0,), (0,)), ((), ())),
                           preferred_element_type=F32)


def _dot_split(a, b):
    a_hi = a.astype(BF16)
    a_lo = (a - a_hi.astype(F32)).astype(BF16)
    b_hi = b.astype(BF16)
    b_lo = (b - b_hi.astype(F32)).astype(BF16)
    mm = functools.partial(jnp.dot, preferred_element_type=F32)
    return mm(a_hi, b_hi) + mm(a_hi, b_lo) + mm(a_lo, b_hi)


def _dot_01(sel, x):
    hi = x.astype(BF16)
    rest = x - hi.astype(F32)
    mid = rest.astype(BF16)
    lo = (rest - mid.astype(F32)).astype(BF16)
    mm = functools.partial(jnp.dot, preferred_element_type=F32)
    return mm(sel, hi) + mm(sel, mid) + mm(sel, lo)


def _dot_f32(a, b):
    return jnp.dot(a, b, precision=HIGHEST, preferred_element_type=F32)


def _rms(x, w):
    return x * lax.rsqrt(jnp.mean(x * x, axis=-1, keepdims=True) + EPS) * w


def _silu(x):
    return x * jax.nn.sigmoid(x)


def _div_pow2(x, n):
    assert n & (n - 1) == 0
    return jnp.right_shift(x, n.bit_length() - 1)


def _mod_pow2(x, n):
    assert n & (n - 1) == 0
    return jnp.bitwise_and(x, n - 1)


def _mod_kernel(c_ref, w_ref, b_ref, o_ref):
    c = c_ref[...]
    o_ref[0] = _dot(_silu(c), w_ref[0]) + b_ref[0]


def _modulation(c, w_mod, b_mod):
    depth, d, n = w_mod.shape
    bsz = c.shape[0]
    rows = 8
    assert bsz <= rows
    tn = n // 8
    c_pad = jnp.pad(c, ((0, rows - bsz), (0, 0)))
    return pl.pallas_call(
        _mod_kernel,
        grid=(depth, n // tn),
        in_specs=[
            pl.BlockSpec((rows, d), lambda l, j: (0, 0)),
            pl.BlockSpec((1, d, tn), lambda l, j: (l, 0, j)),
            pl.BlockSpec((1, 1, tn), lambda l, j: (l, 0, j)),
        ],
        out_specs=pl.BlockSpec((1, rows, tn), lambda l, j: (l, 0, j)),
        out_shape=jax.ShapeDtypeStruct((depth, rows, n), F32),
        compiler_params=_params(("arbitrary", "arbitrary")),
        name="adaln_mod",
    )(c_pad, w_mod, b_mod.reshape(depth, 1, n))


def _mod_specs(sub, d):
    return [pl.BlockSpec((8, d), functools.partial(lambda b, i, j: (0, j), j=3 * sub + k)) for k in range(3)]


def _ffn_kernel(x_ref, sh_ref, sc_ref, gt_ref, npre_ref, npost_ref, win_ref, wout_ref, o_ref, acc_ref,
                *, n_chunks, tf):
    b = pl.program_id(0)
    sh = sh_ref[pl.ds(b, 1), :]
    sc = sc_ref[pl.ds(b, 1), :]
    gt = gt_ref[pl.ds(b, 1), :]
    f = wout_ref.shape[0]
    x = x_ref[0]
    h = (_rms(x, npre_ref[...]) * (1.0 + sc) + sh).astype(BF16)
    for ci in range(n_chunks):
        lo, hi = ci * tf, (ci + 1) * tf
        gate = _dot(h, win_ref[:, lo:hi])
        up = _dot(h, win_ref[:, f + lo:f + hi])
        part = _dot(_silu(gate) * up, wout_ref[lo:hi, :])
        if ci == 0:
            acc_ref[...] = part
        else:
            acc_ref[...] += part
    o_ref[0] = x + (0.5 * gt) * _rms(acc_ref[...], npost_ref[...])


def _ffn(x, mod, sub, npre, npost, w_in, w_out, layer, tm):
    bsz, seq, d = x.shape
    f = w_out.shape[1]
    tf = 256
    assert f % tf == 0 and seq % tm == 0
    n = f // tf
    return pl.pallas_call(
        functools.partial(_ffn_kernel, n_chunks=n, tf=tf),
        grid=(bsz, seq // tm),
        in_specs=[pl.BlockSpec((1, tm, d), lambda b, i: (b, i, 0))] + _mod_specs(sub, d) + [
            pl.BlockSpec((1, d), lambda b, i: (0, 0)),
            pl.BlockSpec((1, d), lambda b, i: (0, 0)),
            _resident((None, d, 2 * f), lambda b, i: (layer, 0, 0)),
            _resident((None, f, d), lambda b, i: (layer, 0, 0)),
        ],
        out_specs=pl.BlockSpec((1, tm, d), lambda b, i: (b, i, 0)),
        out_shape=jax.ShapeDtypeStruct(x.shape, F32),
        scratch_shapes=[pltpu.VMEM((tm, d), F32)],
        compiler_params=_params(("arbitrary", "arbitrary")),
        name="ffn",
    )(x, mod, mod, mod, npre.reshape(1, d), npost.reshape(1, d), w_in, w_out)


def _mix_in_kernel(x_ref, sh_ref, sc_ref, npre_ref, w_ref, wut_ref, conv_ref, alog_ref, dtb_ref,
                   q_ref, k_ref, v_ref, z_ref, ut_ref, bg_ref, ext_ref, *, heads, gw):
    b = pl.program_id(0)
    i = pl.program_id(1)
    tm = x_ref.shape[1]
    x = x_ref[0]
    sh = sh_ref[pl.ds(b, 1), :]
    sc = sc_ref[pl.ds(b, 1), :]
    inv_rms = lax.rsqrt(jnp.mean(x * x, axis=-1, keepdims=True) + EPS)
    h = (x * inv_rms * (npre_ref[...] * (1.0 + sc)) + sh).astype(BF16)
    proj = lambda lo, hi: jnp.dot(h, w_ref[:, lo:hi], preferred_element_type=F32)

    @pl.when(i == 0)
    def _():
        ext_ref[0:8, :] = jnp.zeros((8, 3 * gw), F32)

    for blk, ref in enumerate((q_ref, k_ref, v_ref)):
        cols = slice(blk * gw, (blk + 1) * gw)
        pj = proj(blk * gw, (blk + 1) * gw)
        tap = lambda j: conv_ref[j:j + 1, cols]
        body = tap(CONV_K - 1) * pj
        for dist in range(1, CONV_K):
            body = body + tap(CONV_K - 1 - dist) * pltpu.roll(pj, dist, axis=0)
        ext_ref[8:16, cols] = pj[0:8]
        head = tap(0) * ext_ref[pl.ds(8 - (CONV_K - 1), 8), cols]
        for j in range(1, CONV_K):
            head = head + tap(j) * ext_ref[pl.ds(8 - (CONV_K - 1) + j, 8), cols]
        ext_ref[0:8, cols] = pj[tm - 8:tm]
        act = _silu(jnp.concatenate([head, body[8:]], axis=0))
        if ref is v_ref:
            ref[0] = act
        else:
            for hd in range(heads):
                t = act[:, hd * HEAD_DIM:(hd + 1) * HEAD_DIM]
                ref[0, :, hd * HEAD_DIM:(hd + 1) * HEAD_DIM] = t * lax.rsqrt(
                    jnp.sum(t * t, axis=-1, keepdims=True) + EPS)
    z_ref[0] = proj(3 * gw, 4 * gw)

    ut = lax.dot_general(wut_ref[...], h, (((1,), (1,)), ((), ())), preferred_element_type=F32)
    for grp in range(ut_ref.shape[0]):
        for cc in range(tm // S5_T):
            ut_ref[grp, 0, cc] = ut[grp * S5_GROUP:(grp + 1) * S5_GROUP, cc * S5_T:(cc + 1) * S5_T]

    ba = proj(4 * gw, 4 * gw + BG_LANES)
    beta = jax.nn.sigmoid(ba)
    t = ba + dtb_ref[...]
    softplus = jnp.maximum(t, 0.0) + jnp.log1p(jnp.exp(-jnp.abs(t)))
    g = -jnp.exp(alog_ref[...]) * softplus
    lane = lax.broadcasted_iota(jnp.int32, ba.shape, 1)
    bg_ref[0] = jnp.where(lane < heads, beta, g)


def _mix_in(x, mod, npre, w_in, conv_w, a_log, dt_bias, tm):
    bsz, seq, d = x.shape
    gw = d // 2
    sw = d - gw
    heads = gw // HEAD_DIM
    groups = sw // S5_GROUP
    assert 2 * heads <= BG_LANES and seq % tm == 0 and tm % S5_T == 0
    nc = seq // S5_T
    qkv_w, z_w, beta_w, a_w, u_w = jnp.split(
        w_in, [3 * gw, 4 * gw, 4 * gw + heads, 4 * gw + 2 * heads], axis=1)
    pad = jnp.zeros((d, BG_LANES - 2 * heads), w_in.dtype)
    w = jnp.concatenate([qkv_w, z_w, beta_w, a_w, pad], axis=1).astype(BF16)
    wut = u_w.T.astype(BF16)
    n = w.shape[1]
    lane_pad = (0, BG_LANES - 2 * heads)
    alog = jnp.pad(jnp.concatenate([jnp.zeros_like(a_log), a_log]), lane_pad).reshape(1, BG_LANES)
    dtb = jnp.pad(jnp.concatenate([jnp.zeros_like(dt_bias), dt_bias]), lane_pad).reshape(1, BG_LANES)
    tok = lambda width: pl.BlockSpec((1, tm, width), lambda b, i: (b, i, 0))
    shp = lambda width: jax.ShapeDtypeStruct((bsz, seq, width), F32)
    return pl.pallas_call(
        functools.partial(_mix_in_kernel, heads=heads, gw=gw),
        grid=(bsz, seq // tm),
        in_specs=[tok(d)] + _mod_specs(1, d)[:2] + [
            pl.BlockSpec((1, d), lambda b, i: (0, 0)),
            _resident((d, n), lambda b, i: (0, 0)),
            _resident((sw, d), lambda b, i: (0, 0)),
            pl.BlockSpec((CONV_K, 3 * gw), lambda b, i: (0, 0)),
            pl.BlockSpec((1, BG_LANES), lambda b, i: (0, 0)),
            pl.BlockSpec((1, BG_LANES), lambda b, i: (0, 0)),
        ],
        out_specs=[tok(gw), tok(gw), tok(gw), tok(gw),
                   pl.BlockSpec((groups, 1, tm // S5_T, S5_GROUP, S5_T), lambda b, i: (0, b, i, 0, 0)),
                   tok(BG_LANES)],
        out_shape=[shp(gw), shp(gw), shp(gw), shp(gw),
                   jax.ShapeDtypeStruct((groups, bsz, nc, S5_GROUP, S5_T), F32),
                   shp(BG_LANES)],
        scratch_shapes=[pltpu.VMEM((16, 3 * gw), F32)],
        compiler_params=_params(("arbitrary", "arbitrary")),
        name="mix_in",
    )(x, mod, mod, npre.reshape(1, d), w, wut, conv_w, alog, dtb)


def _gdn_kernel(q_ref, k_ref, v_ref, bg_ref, z_ref, nw_ref, y_ref,
                s_ref, sol_ref, attn_ref, ks_ref, qs_ref, tot_ref, *, heads, n_chunks):
    c = GDN_CHUNK
    step = pl.program_id(1)
    slot_w = lax.rem(step, 2)
    slot_r = 1 - slot_w

    @pl.when(step == 0)
    def _():
        s_ref[...] = jnp.zeros_like(s_ref)
        for ref in (sol_ref, attn_ref, ks_ref, qs_ref, tot_ref):
            ref[1] = jnp.zeros(ref.shape[1:], F32)

    row = lax.broadcasted_iota(jnp.int32, (c, c), 0)
    col = lax.broadcasted_iota(jnp.int32, (c, c), 1)
    lower_incl = row >= col
    strict = row > col
    eye = (row == col).astype(F32)
    sum_rows = lax.broadcasted_iota(jnp.int32, (c + HEAD_DIM, c), 0)
    sum_cols = lax.broadcasted_iota(jnp.int32, (c + HEAD_DIM, c), 1)
    cum_and_total = ((sum_rows >= sum_cols) | (sum_rows >= c)).astype(BF16)
    ones_c = jnp.ones((c, c), BF16)
    wide_row = lax.broadcasted_iota(jnp.int32, (c, HEAD_DIM), 0)
    wide_col = lax.broadcasted_iota(jnp.int32, (c, HEAD_DIM), 1)
    utri_wide = ((wide_row <= wide_col) & (wide_col < c)).astype(F32)
    scale = HEAD_DIM ** -0.5
    nw = nw_ref[...]

    pairs = [(ci, hd) for ci in range(n_chunks) for hd in range(heads)]
    rows_of = lambda ci: slice(ci * c, (ci + 1) * c)
    lanes_of = lambda hd: slice(hd * HEAD_DIM, (hd + 1) * HEAD_DIM)
    lane1 = lambda x, j: x[:, j:j + 1]

    pid = {p: n for n, p in enumerate(pairs)}
    t = {}

    def intra_setup():
        bgs = [bg_ref[0, rows_of(ci), :] for ci in range(n_chunks)]
        sums = [_dot_01(cum_and_total, bg) for bg in bgs]
        gc_rows = [_dot_01(ones_c, jnp.concatenate(
            [lane1(bg, heads + hd) * utri_wide for hd in range(heads)], axis=-1)) for bg in bgs]
        for ci in range(n_chunks):
            tot_ref[slot_w, ci] = sums[ci][c:]
        for p in pairs:
            ci, hd = p
            t["beta", p] = lane1(bgs[ci], hd)
            t["gc", p] = lane1(sums[ci][:c], heads + hd)
            gl = lane1(sums[ci][c:], heads + hd)
            gc_row = gc_rows[ci][:, hd * HEAD_DIM:hd * HEAD_DIM + c]
            t["decay", p] = jnp.exp(jnp.where(lower_incl, t["gc", p] - gc_row, -jnp.inf))
            t["q", p] = q_ref[0, rows_of(ci), lanes_of(hd)] * scale
            t["k", p] = k_ref[0, rows_of(ci), lanes_of(hd)]
            t["kb", p] = t["k", p] * t["beta", p]
            t["eg", p] = jnp.exp(t["gc", p])
            ks_ref[slot_w, pid[p]] = t["k", p] * jnp.exp(gl[:c] - t["gc", p])
            qs_ref[slot_w, pid[p]] = t["q", p] * t["eg", p]

    def intra_a_mat():
        for p in pairs:
            t["a", p] = jnp.where(strict, -(_dot_nt(t["kb", p], t["k", p]) * t["decay", p]), 0.0)
            t["pow", p] = t["a", p]
            t["inv", p] = eye + t["a", p]

    def intra_square():
        for p in pairs:
            t["pow", p] = _dot(t["pow", p], t["pow", p])

    def intra_extend():
        for p in pairs:
            t["inv", p] = t["inv", p] + _dot(t["inv", p], t["pow", p])

    def intra_resid():
        for p in pairs:
            t["resid", p] = (eye - t["inv", p]) + _dot_split(t["a", p], t["inv", p])

    def intra_newton():
        for p in pairs:
            t["inv", p] = t["inv", p] + _dot(t["inv", p], t["resid", p])

    def intra_solve():
        for p in pairs:
            ci, hd = p
            rhs = jnp.concatenate([v_ref[0, rows_of(ci), lanes_of(hd)] * t["beta", p],
                                   t["kb", p] * t["eg", p]], axis=-1)
            sol_ref[slot_w, pid[p]] = _dot(t["inv", p], rhs)

    def intra_attn():
        for p in pairs:
            attn_ref[slot_w, pid[p]] = _dot_nt(t["q", p], t["k", p]) * t["decay", p]

    intra = [intra_setup, intra_a_mat]
    for _ in range(int(math.log2(c)) - 2):
        intra += [intra_square, intra_extend]
    intra += [intra_resid, intra_newton, intra_solve, intra_attn]

    r = {"state": [s_ref[hd] for hd in range(heads)]}

    def inter_v_new(ci):
        def run():
            r["sb"] = [r["state"][hd].astype(BF16) for hd in range(heads)]
            r["v_new"] = []
            for hd in range(heads):
                sol = sol_ref[slot_r, pid[ci, hd]]
                r["v_new"].append(sol[:, :HEAD_DIM] - _dot(sol[:, HEAD_DIM:], r["sb"][hd]))
        return run

    def inter_out(ci):
        def run():
            new_state = []
            for hd in range(heads):
                n = pid[ci, hd]
                o = _dot(qs_ref[slot_r, n], r["sb"][hd]) + _dot(attn_ref[slot_r, n], r["v_new"][hd])
                gl = lane1(tot_ref[slot_r, ci], heads + hd)
                new_state.append(r["state"][hd] * jnp.exp(gl) + _dot_tn(ks_ref[slot_r, n], r["v_new"][hd]))
                z = z_ref[0, rows_of(ci), lanes_of(hd)]
                y_ref[0, rows_of(ci), lanes_of(hd)] = (_rms(o, nw) * _silu(z)).astype(BF16)
            r["state"] = new_state
        return run

    inter = []
    for ci in range(n_chunks):
        inter += [inter_v_new(ci), inter_out(ci)]

    for n in range(max(len(intra), len(inter))):
        if n < len(intra):
            intra[n]()
        if n < len(inter):
            inter[n]()
    for hd in range(heads):
        s_ref[hd] = r["state"][hd]


def _gdn(q, k, v, z, bg, norm_w, tl):
    bsz, seq, gw = q.shape
    heads = gw // HEAD_DIM
    assert seq % tl == 0 and tl % GDN_CHUNK == 0
    n_tiles = seq // tl
    n_chunks = tl // GDN_CHUNK
    n_pairs = n_chunks * heads
    cur = lambda width: pl.BlockSpec((1, tl, width), lambda b, i: (b, jnp.minimum(i, n_tiles - 1), 0))
    prev = lambda width: pl.BlockSpec((1, tl, width), lambda b, i: (b, jnp.maximum(i - 1, 0), 0))
    slots = lambda *shape: pltpu.VMEM((2,) + shape, F32)
    return pl.pallas_call(
        functools.partial(_gdn_kernel, heads=heads, n_chunks=n_chunks),
        grid=(bsz, n_tiles + 1),
        in_specs=[cur(gw), cur(gw), cur(gw), cur(BG_LANES), prev(gw),
                  pl.BlockSpec((1, HEAD_DIM), lambda b, i: (0, 0))],
        out_specs=prev(gw),
        out_shape=jax.ShapeDtypeStruct((bsz, seq, gw), BF16),
        scratch_shapes=[pltpu.VMEM((heads, HEAD_DIM, HEAD_DIM), F32),
                        slots(n_pairs, GDN_CHUNK, 2 * HEAD_DIM),
                        slots(n_pairs, GDN_CHUNK, GDN_CHUNK),
                        slots(n_pairs, GDN_CHUNK, HEAD_DIM),
                        slots(n_pairs, GDN_CHUNK, HEAD_DIM),
                        slots(n_chunks, HEAD_DIM, BG_LANES)],
        compiler_params=_params(("arbitrary", "arbitrary")),
        name="gdn",
    )(q, k, v, bg, z, norm_w.reshape(1, HEAD_DIM))


def _s5_param_kernel(are_ref, aim_ref, arc_ref, aic_ref, ldt_ref, brt_ref, bit_ref, cr_ref, ci_ref,
                     crt_ref, cit_ref, kft_ref, p_ref, g1t_ref, mul_ref, *, n_levels):
    t_len = S5_T
    ns = S5_STATE
    hw = S5_GROUP
    dt = jnp.exp(ldt_ref[0])

    ar = jnp.minimum(are_ref[0], -1e-4)
    ai = aim_ref[0]
    mag = jnp.exp(dt * ar)
    abar_re = mag * jnp.cos(dt * ai)
    abar_im = mag * jnp.sin(dt * ai)
    denom = ar * ar + ai * ai
    zr = abar_re - 1.0
    zi = abar_im
    fr = (zr * ar + zi * ai) / denom
    fi = (zi * ar - zr * ai) / denom
    brt = brt_ref[0]
    bit = bit_ref[0]
    bbar_re_t = fr * brt - fi * bit
    bbar_im_t = fr * bit + fi * brt

    tau_rev = (t_len - 1 - lax.broadcasted_iota(jnp.int32, (t_len, ns), 0)).astype(F32)
    rev_mag = jnp.exp(tau_rev * (dt * ar))
    rev_re = rev_mag * jnp.cos(tau_rev * (dt * ai))
    rev_im = rev_mag * jnp.sin(tau_rev * (dt * ai))
    for hp in range(hw):
        b_r = bbar_re_t[hp:hp + 1, :]
        b_i = bbar_im_t[hp:hp + 1, :]
        p_ref[0, hp * t_len:(hp + 1) * t_len, :] = jnp.concatenate(
            [rev_re * b_r - rev_im * b_i, rev_re * b_i + rev_im * b_r], axis=-1).astype(BF16)

    ar_c = jnp.minimum(arc_ref[0], -1e-4)
    ai_c = aic_ref[0]
    tau = lax.broadcasted_iota(jnp.int32, (ns, t_len), 1).astype(F32)

    def lam_pow(t):
        m = jnp.exp(t * (dt * ar_c))
        return m * jnp.cos(t * (dt * ai_c)), m * jnp.sin(t * (dt * ai_c))

    pow_re, pow_im = lam_pow(tau)
    rows = hw * hw
    r_idx = lax.broadcasted_iota(jnp.int32, (rows, hw), 0)
    c_idx = lax.broadcasted_iota(jnp.int32, (rows, hw), 1)
    pick_h = (_mod_pow2(r_idx, hw) == c_idx).astype(F32)
    pick_hp = (_div_pow2(r_idx, hw) == c_idx).astype(F32)
    c_r = _dot_f32(pick_h, cr_ref[0])
    c_i = _dot_f32(pick_h, ci_ref[0])
    b_r = _dot_f32(pick_hp, bbar_re_t)
    b_i = _dot_f32(pick_hp, bbar_im_t)
    cb = jnp.concatenate([c_r * b_r - c_i * b_i, -(c_r * b_i + c_i * b_r)], axis=-1)
    kft_ref[0] = _dot_f32(cb, jnp.concatenate([pow_re, pow_im], axis=0))

    lam_re = pow_re[:, 1:2]
    lam_im = pow_im[:, 1:2]
    nxt_re = pow_re * lam_re - pow_im * lam_im
    nxt_im = pow_re * lam_im + pow_im * lam_re
    crt = crt_ref[0]
    cit = cit_ref[0]
    for h in range(hw):
        c_r = crt[:, h:h + 1]
        c_i = cit[:, h:h + 1]
        g1t_ref[0, :, h * t_len:(h + 1) * t_len] = jnp.concatenate(
            [c_r * nxt_re - c_i * nxt_im, -(c_r * nxt_im + c_i * nxt_re)], axis=0)

    step_mag = jnp.exp(float(t_len) * (dt * ar))
    step_re = step_mag * jnp.cos(float(t_len) * (dt * ai))
    step_im = step_mag * jnp.sin(float(t_len) * (dt * ai))
    for lvl in range(n_levels):
        mul_ref[0, 2 * lvl:2 * lvl + 1, :] = jnp.concatenate([step_re, step_re], axis=-1)
        mul_ref[0, 2 * lvl + 1:2 * lvl + 2, :] = jnp.concatenate([-step_im, step_im], axis=-1)
        step_re, step_im = step_re * step_re - step_im * step_im, 2.0 * step_re * step_im


def _s5_main_kernel(ut_ref, kft_ref, p_ref, g1t_ref, mul_ref, d_ref, yt_ref, m_ref, a_ref, abf_ref,
                    *, n_chunks, n_levels):
    t_len = S5_T
    hw = S5_GROUP
    rows = a_ref.shape[0]

    causal = (lax.broadcasted_iota(jnp.int32, (t_len, t_len), 1)
              >= lax.broadcasted_iota(jnp.int32, (t_len, t_len), 0))
    strip = 2 * t_len
    n_strips = hw // 2

    def build_strip(k, slot):
        for j in range(2):
            h = 2 * k + j
            for hp in range(hw):
                k_row = kft_ref[0, pl.ds(hp * hw + h, 1), :]
                blk = pltpu.roll(jnp.broadcast_to(k_row, (t_len, t_len)), 0, 1, stride=1, stride_axis=0)
                m_ref[slot, hp * t_len:(hp + 1) * t_len, j * t_len:(j + 1) * t_len] = (
                    jnp.where(causal, blk, 0.0).astype(BF16))

    for hp in range(hw):
        a_ref[:, hp * t_len:(hp + 1) * t_len] = ut_ref[0, pl.ds(hp, rows, stride=hw), :]
    a = a_ref[...]
    abf_ref[...] = a.astype(BF16)
    state = jnp.dot(abf_ref[...], p_ref[0], preferred_element_type=F32)
    chunk = _mod_pow2(lax.broadcasted_iota(jnp.int32, state.shape, 0), n_chunks)
    for lvl in range(n_levels):
        dist = 2 ** lvl
        prev = jnp.where(chunk >= dist, pltpu.roll(state, dist, axis=0), 0.0)
        state = (state + prev * mul_ref[0, 2 * lvl:2 * lvl + 1, :]
                 + pltpu.roll(prev, S5_STATE, axis=1) * mul_ref[0, 2 * lvl + 1:2 * lvl + 2, :])
    incoming = jnp.where(chunk >= 1, pltpu.roll(state, 1, axis=0), 0.0)
    incoming = incoming.astype(BF16)

    build_strip(0, 0)

    def produce(k, carry):
        slot = lax.rem(k, 2)
        c0 = pl.multiple_of(k * strip, strip)
        y = (jnp.dot(abf_ref[...], m_ref[slot], preferred_element_type=F32)
             + jnp.dot(incoming, g1t_ref[0, :, pl.ds(c0, strip)].astype(BF16), preferred_element_type=F32)
             + d_ref[0, :, pl.ds(c0, strip)] * a_ref[:, pl.ds(c0, strip)])
        y = jax.nn.gelu(y)
        for j in range(2):
            yt_ref[0, pl.ds(2 * k + j, rows, stride=hw), :] = y[:, j * t_len:(j + 1) * t_len]
        build_strip(jnp.minimum(k + 1, n_strips - 1), 1 - slot)
        return carry

    lax.fori_loop(0, n_strips, produce, 0)


def _s5(ut, a_re, a_im, log_dt, b_re, b_im, c_re, c_im, d_skip):
    groups, bsz, nc, hw, t_len = ut.shape
    n_levels = max(1, int(math.ceil(math.log2(nc))))
    tw = t_len * hw
    ns = S5_STATE
    grp = lambda *tail: pl.BlockSpec((1,) + tail, lambda g: (g,) + (0,) * len(tail))
    col = lambda a: a.reshape(groups, ns, 1)
    row = lambda a: a.reshape(groups, 1, ns)

    kft, p_op, g1t, mul = pl.pallas_call(
        functools.partial(_s5_param_kernel, n_levels=n_levels),
        grid=(groups,),
        in_specs=[grp(1, ns), grp(1, ns), grp(ns, 1), grp(ns, 1), grp(1, 1),
                  grp(hw, ns), grp(hw, ns), grp(hw, ns), grp(hw, ns), grp(ns, hw), grp(ns, hw)],
        out_specs=[grp(hw * hw, t_len), grp(tw, 2 * ns), grp(2 * ns, tw), grp(2 * n_levels, 2 * ns)],
        out_shape=[jax.ShapeDtypeStruct((groups, hw * hw, t_len), F32),
                   jax.ShapeDtypeStruct((groups, tw, 2 * ns), BF16),
                   jax.ShapeDtypeStruct((groups, 2 * ns, tw), F32),
                   jax.ShapeDtypeStruct((groups, 2 * n_levels, 2 * ns), F32)],
        compiler_params=_params(("arbitrary",)),
        name="s5_params",
    )(row(a_re), row(a_im), col(a_re), col(a_im), log_dt.reshape(groups, 1, 1),
      b_re.transpose(0, 2, 1), b_im.transpose(0, 2, 1), c_re, c_im,
      c_re.transpose(0, 2, 1), c_im.transpose(0, 2, 1))

    rows = bsz * nc
    d_row = jnp.repeat(d_skip, t_len, axis=1).reshape(groups, 1, tw)
    yt = pl.pallas_call(
        functools.partial(_s5_main_kernel, n_chunks=nc, n_levels=n_levels),
        grid=(groups,),
        in_specs=[grp(rows * hw, t_len), grp(hw * hw, t_len), grp(tw, 2 * ns), grp(2 * ns, tw),
                  grp(2 * n_levels, 2 * ns), grp(1, tw)],
        out_specs=grp(rows * hw, t_len),
        out_shape=jax.ShapeDtypeStruct((groups, rows * hw, t_len), F32),
        scratch_shapes=[pltpu.VMEM((2, tw, 2 * t_len), BF16), pltpu.VMEM((rows, tw), F32),
                        pltpu.VMEM((rows, tw), BF16)],
        compiler_params=_params(("arbitrary",)),
        name="s5_main",
    )(ut.reshape(groups, rows * hw, t_len), kft, p_op, g1t, mul, d_row)
    return yt.reshape(groups, bsz, nc, hw, t_len)


def _mix_out_kernel(x_ref, gt_ref, yg_ref, yst_ref, wglut_ref, wout_ref, npost_ref, o_ref, yt_ref):
    b = pl.program_id(0)
    gt = gt_ref[pl.ds(b, 1), :]
    for grp in range(yst_ref.shape[0]):
        for cc in range(yst_ref.shape[2]):
            yt_ref[grp * S5_GROUP:(grp + 1) * S5_GROUP, cc * S5_T:(cc + 1) * S5_T] = yst_ref[grp, 0, cc]
    yt = yt_ref[...]
    st = yt * jax.nn.sigmoid(_dot(wglut_ref[...], yt))
    gw = yg_ref.shape[2]
    y = _dot(yg_ref[0], wout_ref[:gw, :]) + _dot_tn(st, wout_ref[gw:, :])
    o_ref[0] = x_ref[0] + gt * _rms(y, npost_ref[...])


def _mix_out(x, mod, y_gdn, yt_s5, w_glu, w_out, npost, tm):
    bsz, seq, d = x.shape
    gw = y_gdn.shape[2]
    groups, _, _, hw, t_len = yt_s5.shape
    sw = groups * hw
    tok = lambda width: pl.BlockSpec((1, tm, width), lambda b, i: (b, i, 0))
    return pl.pallas_call(
        _mix_out_kernel,
        grid=(bsz, seq // tm),
        in_specs=[tok(d), _mod_specs(1, d)[2], tok(gw),
                  pl.BlockSpec((groups, 1, tm // t_len, hw, t_len), lambda b, i: (0, b, i, 0, 0)),
                  _resident((sw, sw), lambda b, i: (0, 0)),
                  _resident((gw + sw, d), lambda b, i: (0, 0)),
                  pl.BlockSpec((1, d), lambda b, i: (0, 0))],
        out_specs=tok(d),
        out_shape=jax.ShapeDtypeStruct(x.shape, F32),
        scratch_shapes=[pltpu.VMEM((sw, tm), F32)],
        compiler_params=_params(("arbitrary", "arbitrary")),
        name="mix_out",
    )(x, mod, y_gdn, yt_s5, w_glu.T.astype(BF16), w_out.astype(BF16), npost.reshape(1, d))


def kernel(x, c, w_mod, b_mod, ff1_norm_pre, ff1_norm_post, ff1_w_in, ff1_w_out, mix_norm_pre, mix_norm_post, mix_w_in, conv_w, a_log, dt_bias, gdn_norm_w, s5_a_re, s5_a_im, s5_log_dt, s5_b_re, s5_b_im, s5_c_re, s5_c_im, s5_d, s5_w_glu, mix_w_out, ff2_norm_pre, ff2_norm_post, ff2_w_in, ff2_w_out):
    depth = w_mod.shape[0]
    seq = x.shape[1]
    tm = min(1024, seq)
    tm_ffn = min(512, seq)
    tm_out = min(1024, seq)
    tl = min(512, seq)
    mods = _modulation(c, w_mod, b_mod)
    for l in range(depth):
        mod = mods[l]
        x = _ffn(x, mod, 0, ff1_norm_pre[l], ff1_norm_post[l], ff1_w_in, ff1_w_out, l, tm_ffn)
        q, k, v, z, ut, bg = _mix_in(x, mod, mix_norm_pre[l], mix_w_in[l], conv_w[l], a_log[l], dt_bias[l], tm)
        y_gdn = _gdn(q, k, v, z, bg, gdn_norm_w[l], tl)
        yt_s5 = _s5(ut, s5_a_re[l], s5_a_im[l], s5_log_dt[l], s5_b_re[l], s5_b_im[l],
                    s5_c_re[l], s5_c_im[l], s5_d[l])
        x = _mix_out(x, mod, y_gdn, yt_s5, s5_w_glu[l], mix_w_out[l], mix_norm_post[l], tm_out)
        x = _ffn(x, mod, 2, ff2_norm_pre[l], ff2_norm_post[l], ff2_w_in, ff2_w_out, l, tm_ffn)
    return x
```
